```python
import math
import jax, jax.numpy as jnp
from jax import lax
import numpy as np

D_MODEL = 4096
BATCH = 4
SEQ = 2048
DEPTH = 2
DEC_BATCH = 128
DEC_SEQ = 1
PAST_LEN = 16384
PAGE_SIZE = 128

N_EVEN = (DEPTH + 1) // 2
N_ODD = DEPTH // 2
W_A = D_MODEL // 2
W_B = D_MODEL // 2
CONV_A = 3
CONV_B = 4
LRU_HEADS = 16
LRU_HEAD_DIM = W_B // LRU_HEADS
LRU_C = 8.0
IN_EVEN = 3 * W_A + 2 * W_B
GLA_HEADS = 8
GLA_DK_TOT = D_MODEL // 2
GLA_DV_TOT = D_MODEL
GLA_DK = GLA_DK_TOT // GLA_HEADS
GLA_DV = GLA_DV_TOT // GLA_HEADS
GLA_GATE_RANK = 16
GLA_GATE_NORM = 16.0
GLA_CHUNK = 16
IN_ODD = 2 * GLA_DK_TOT + 2 * GLA_DV_TOT + GLA_GATE_RANK
N_GROUPS = 8
EXPERTS_PER_GROUP = 8
N_EXPERTS = N_GROUPS * EXPERTS_PER_GROUP
TOP_K_IN_GROUP = 2
D_EXPERT = 1024
EPS = 1e-6

kernel_name = 'hybrid_shortconv_rglru_gla_hmoe_step'


def rmsnorm(x, g):
    xf = x.astype(jnp.float32)
    y = xf * lax.rsqrt(jnp.mean(xf * xf, axis=-1, keepdims=True) + EPS)
    return (y * g.astype(jnp.float32)).astype(x.dtype)


def causal_conv(u, buf, w):
    K = w.shape[0]
    T = u.shape[1]
    xc = jnp.concatenate([buf.astype(u.dtype), u], axis=1)
    y = sum(xc[:, k:k + T] * w[k] for k in range(K))
    return y, xc[:, T:]


def rglru(x, r, i, lam, h0):
    log_a = -LRU_C * r * jax.nn.softplus(-lam.astype(jnp.float32))
    a = jnp.exp(log_a)
    b = jnp.sqrt(-jnp.expm1(2.0 * log_a)) * (i * x.astype(jnp.float32))

    def comb(lhs, rhs):
        a1, b1 = lhs
        a2, b2 = rhs
        return a1 * a2, a2 * b1 + b2

    a_cum, b_cum = lax.associative_scan(comb, (a, b), axis=1)
    h = a_cum * h0.astype(jnp.float32)[:, None, :] + b_cum
    return h, h[:, -1]


def even_mixer(h, buf_a, buf_b, h0, w_in, conv_a_w, conv_b_w, conv_b_b,
               lru_wa, lru_ba, lru_wx, lru_bx, lru_lambda, w_out):
    bsz, T, _ = h.shape
    z = h @ w_in
    gb, gc, xa, xr, gr = jnp.split(z, [W_A, 2 * W_A, 3 * W_A, 3 * W_A + W_B], axis=-1)
    v, new_a = causal_conv(gc * xa, buf_a, conv_a_w)
    y_a = gb * v
    xc, new_b = causal_conv(xr, buf_b, conv_b_w)
    xc = xc + conv_b_b
    xh = xc.reshape(bsz, T, LRU_HEADS, LRU_HEAD_DIM)
    r = jax.nn.sigmoid((jnp.einsum('bthi,hij->bthj', xh, lru_wa).reshape(bsz, T, W_B) + lru_ba).astype(jnp.float32))
    i = jax.nn.sigmoid((jnp.einsum('bthi,hij->bthj', xh, lru_wx).reshape(bsz, T, W_B) + lru_bx).astype(jnp.float32))
    hseq, h_last = rglru(xc, r, i, lru_lambda, h0)
    y_b = jax.nn.gelu(gr) * hseq.astype(h.dtype)
    y = jnp.concatenate([y_a, y_b], axis=-1) @ w_out
    return y, new_a, new_b, h_last


def gla_chunked(q, k, v, log_a, S0):
    bsz, T, H, _ = q.shape
    pad = (-T) % GLA_CHUNK
    n = (T + pad) // GLA_CHUNK

    def prep(t):
        t = jnp.pad(t.astype(jnp.float32), ((0, 0), (0, pad), (0, 0), (0, 0)))
        return t.reshape(bsz, n, GLA_CHUNK, H, t.shape[-1]).transpose(1, 0, 3, 2, 4)

    qs, ks, vs, gs = prep(q), prep(k), prep(v), prep(log_a)
    mask = jnp.tril(jnp.ones((GLA_CHUNK, GLA_CHUNK), dtype=bool))

    def step(S, inp):
        qc, kc, vc, gc = inp
        b = jnp.cumsum(gc, axis=2)
        b_last = b[:, :, -1:, :]
        qe = qc * jnp.exp(b)
        ke = kc * jnp.exp(-b)
        att = jnp.where(mask, jnp.einsum('bhid,bhjd->bhij', qe, ke), 0.0)
        o = jnp.einsum('bhid,bhde->bhie', qe, S) + jnp.einsum('bhij,bhje->bhie', att, vc)
        S = jnp.exp(b_last[:, :, 0, :])[..., None] * S + jnp.einsum('bhjd,bhje->bhde', kc * jnp.exp(b_last - b), vc)
        return S, o

    S, os_ = lax.scan(step, S0.astype(jnp.float32), (qs, ks, vs, gs))
    o = os_.transpose(1, 0, 3, 2, 4).reshape(bsz, n * GLA_CHUNK, H, GLA_DV)[:, :T]
    return o, S


def odd_mixer(h, S0, w_in, gla_wa2, gla_ba, gla_norm, w_out):
    bsz, T, _ = h.shape
    z = h @ w_in
    q, k, v, g, gr = jnp.split(z, [GLA_DK_TOT, 2 * GLA_DK_TOT, 2 * GLA_DK_TOT + GLA_DV_TOT,
                                   2 * GLA_DK_TOT + 2 * GLA_DV_TOT], axis=-1)
    q = q.reshape(bsz, T, GLA_HEADS, GLA_DK) * (GLA_DK ** -0.5)
    k = k.reshape(bsz, T, GLA_HEADS, GLA_DK)
    v = v.reshape(bsz, T, GLA_HEADS, GLA_DV)
    log_a = jax.nn.log_sigmoid((gr @ gla_wa2 + gla_ba).astype(jnp.float32)) / GLA_GATE_NORM
    log_a = log_a.reshape(bsz, T, GLA_HEADS, GLA_DK)
    o, S = gla_chunked(q, k, v, log_a, S0)
    o = rmsnorm(o, gla_norm).astype(h.dtype)
    o = o * jax.nn.silu(g).reshape(bsz, T, GLA_HEADS, GLA_DV)
    y = o.reshape(bsz, T, GLA_DV_TOT) @ w_out
    return y, S


def hier_moe(h, wrg, brg, wre, bre, wg, wu, wd):
    bsz, T, D = h.shape
    xt = h.reshape(-1, D)
    N = xt.shape[0]
    lg = (xt @ wrg + brg).astype(jnp.float32)
    pg_all = jax.nn.softmax(lg, axis=-1)
    _, g_idx = lax.top_k(lg, 1)
    pg = jnp.take_along_axis(pg_all, g_idx, axis=1)
    le = (xt @ wre + bre).astype(jnp.float32).reshape(N, N_GROUPS, EXPERTS_PER_GROUP)
    le_sel = jnp.take_along_axis(le, g_idx[:, :, None], axis=1)[:, 0]
    top_l, top_i = lax.top_k(le_sel, TOP_K_IN_GROUP)
    w = jax.nn.softmax(top_l, axis=-1) * pg
    eid = (g_idx * EXPERTS_PER_GROUP + top_i).reshape(-1)
    order = jnp.argsort(eid)
    xs = xt[order // TOP_K_IN_GROUP]
    gsz = jnp.bincount(eid, length=N_EXPERTS).astype(jnp.int32)
    a = jax.nn.silu(lax.ragged_dot(xs, wg, gsz)) * lax.ragged_dot(xs, wu, gsz)
    o = lax.ragged_dot(a, wd, gsz)
    o = jnp.zeros_like(o).at[order].set(o)
    y = jnp.sum(o.reshape(N, TOP_K_IN_GROUP, D) * w[..., None].astype(o.dtype), axis=1)
    return y.reshape(bsz, T, D)


def trunk(x, st_a, st_b, st_h, st_s, p):
    new_a, new_b, new_h, new_s = [], [], [], []
    for l in range(DEPTH):
        e = l // 2
        h = rmsnorm(x, p['norm_mix'][l])
        if l % 2 == 0:
            y, na, nb, nh = even_mixer(h, st_a[e], st_b[e], st_h[e], p['w_in_even'][e],
                                       p['conv_a_w'][e], p['conv_b_w'][e], p['conv_b_b'][e],
                                       p['lru_wa'][e], p['lru_ba'][e], p['lru_wx'][e], p['lru_bx'][e],
                                       p['lru_lambda'][e], p['w_out_even'][e])
            new_a.append(na)
            new_b.append(nb)
            new_h.append(nh)
        else:
            y, ns = odd_mixer(h, st_s[e], p['w_in_odd'][e], p['gla_wa2'][e], p['gla_ba'][e],
                              p['gla_norm'][e], p['w_out_odd'][e])
            new_s.append(ns)
        x = x + y.astype(x.dtype)
        m = hier_moe(rmsnorm(x, p['norm_ffn'][l]), p['router_group_w'][l], p['router_group_b'][l],
                     p['router_expert_w'][l], p['router_expert_b'][l],
                     p['moe_w_gate'][l], p['moe_w_up'][l], p['moe_w_down'][l])
        x = x + m.astype(x.dtype)
    y = rmsnorm(x, p['norm_final'])
    return y, jnp.stack(new_a), jnp.stack(new_b), jnp.stack(new_h), jnp.stack(new_s)


def setup_inputs(seed: int = 0) -> dict:
    key = jax.random.key(seed)
    ks = jax.random.split(key, 40)
    f32 = jnp.float32

    def nrm(k, shape, scale):
        return jax.random.normal(k, shape, f32) * scale

    a0 = jax.random.uniform(ks[15], (N_EVEN, W_B), f32, 0.9, 0.999)
    s = a0 ** (1.0 / LRU_C)
    lru_lambda = jnp.log(s) - jnp.log1p(-s)
    return {
        'x_prompt': nrm(ks[0], (BATCH, SEQ, D_MODEL), 1.0),
        'x_sample': nrm(ks[1], (DEC_BATCH, DEC_SEQ, D_MODEL), 1.0),
        'state_conv_a': nrm(ks[2], (N_EVEN, DEC_BATCH, CONV_A - 1, W_A), 1.0),
        'state_conv_b': nrm(ks[3], (N_EVEN, DEC_BATCH, CONV_B - 1, W_B), 1.0),
        'state_lru_h': nrm(ks[4], (N_EVEN, DEC_BATCH, W_B), 0.5),
        'state_gla': nrm(ks[5], (N_ODD, DEC_BATCH, GLA_HEADS, GLA_DK, GLA_DV), 1.0),
        'norm_mix': 1.0 + nrm(ks[6], (DEPTH, D_MODEL), 0.02),
        'norm_ffn': 1.0 + nrm(ks[7], (DEPTH, D_MODEL), 0.02),
        'norm_final': 1.0 + nrm(ks[8], (D_MODEL,), 0.02),
        'w_in_even': nrm(ks[9], (N_EVEN, D_MODEL, IN_EVEN), D_MODEL ** -0.5),
        'conv_a_w': nrm(ks[10], (N_EVEN, CONV_A, W_A), CONV_A ** -0.5),
        'conv_b_w': nrm(ks[11], (N_EVEN, CONV_B, W_B), CONV_B ** -0.5),
        'conv_b_b': nrm(ks[12], (N_EVEN, W_B), 0.02),
        'lru_wa': nrm(ks[13], (N_EVEN, LRU_HEADS, LRU_HEAD_DIM, LRU_HEAD_DIM), LRU_HEAD_DIM ** -0.5),
        'lru_ba': nrm(ks[14], (N_EVEN, W_B), 0.02),
        'lru_wx': nrm(ks[16], (N_EVEN, LRU_HEADS, LRU_HEAD_DIM, LRU_HEAD_DIM), LRU_HEAD_DIM ** -0.5),
        'lru_bx': nrm(ks[17], (N_EVEN, W_B), 0.02),
        'lru_lambda': lru_lambda,
        'w_out_even': nrm(ks[18], (N_EVEN, W_A + W_B, D_MODEL), (W_A + W_B) ** -0.5),
        'w_in_odd': nrm(ks[19], (N_ODD, D_MODEL, IN_ODD), D_MODEL ** -0.5),
        'gla_wa2': nrm(ks[20], (N_ODD, GLA_GATE_RANK, GLA_DK_TOT), GLA_GATE_RANK ** -0.5),
        'gla_ba': nrm(ks[21], (N_ODD, GLA_DK_TOT), 0.02),
        'gla_norm': 1.0 + nrm(ks[22], (N_ODD, GLA_DV), 0.02),
        'w_out_odd': nrm(ks[23], (N_ODD, GLA_DV_TOT, D_MODEL), GLA_DV_TOT ** -0.5),
        'router_group_w': nrm(ks[24], (DEPTH, D_MODEL, N_GROUPS), D_MODEL ** -0.5),
        'router_group_b': nrm(ks[25], (DEPTH, N_GROUPS), 0.01),
        'router_expert_w': nrm(ks[26], (DEPTH, D_MODEL, N_EXPERTS), D_MODEL ** -0.5),
        'router_expert_b': nrm(ks[27], (DEPTH, N_EXPERTS), 0.01),
        'moe_w_gate': nrm(ks[28], (DEPTH, N_EXPERTS, D_MODEL, D_EXPERT), D_MODEL ** -0.5),
        'moe_w_up': nrm(ks[29], (DEPTH, N_EXPERTS, D_MODEL, D_EXPERT), D_MODEL ** -0.5),
        'moe_w_down': nrm(ks[30], (DEPTH, N_EXPERTS, D_EXPERT, D_MODEL), D_EXPERT ** -0.5),
    }


def reference(x_prompt, x_sample, state_conv_a, state_conv_b, state_lru_h, state_gla,
              norm_mix, norm_ffn, norm_final, w_in_even, conv_a_w, conv_b_w, conv_b_b,
              lru_wa, lru_ba, lru_wx, lru_bx, lru_lambda, w_out_even, w_in_odd, gla_wa2,
              gla_ba, gla_norm, w_out_odd, router_group_w, router_group_b, router_expert_w,
              router_expert_b, moe_w_gate, moe_w_up, moe_w_down):
    p = {
        'norm_mix': norm_mix, 'norm_ffn': norm_ffn, 'norm_final': norm_final,
        'w_in_even': w_in_even, 'conv_a_w': conv_a_w, 'conv_b_w': conv_b_w, 'conv_b_b': conv_b_b,
        'lru_wa': lru_wa, 'lru_ba': lru_ba, 'lru_wx': lru_wx, 'lru_bx': lru_bx,
        'lru_lambda': lru_lambda, 'w_out_even': w_out_even,
        'w_in_odd': w_in_odd, 'gla_wa2': gla_wa2, 'gla_ba': gla_ba, 'gla_norm': gla_norm,
        'w_out_odd': w_out_odd,
        'router_group_w': router_group_w, 'router_group_b': router_group_b,
        'router_expert_w': router_expert_w, 'router_expert_b': router_expert_b,
        'moe_w_gate': moe_w_gate, 'moe_w_up': moe_w_up, 'moe_w_down': moe_w_down,
    }
    pb = x_prompt.shape[0]
    z_a = jnp.zeros((N_EVEN, pb, CONV_A - 1, W_A), jnp.float32)
    z_b = jnp.zeros((N_EVEN, pb, CONV_B - 1, W_B), jnp.float32)
    z_h = jnp.zeros((N_EVEN, pb, W_B), jnp.float32)
    z_s = jnp.zeros((N_ODD, pb, GLA_HEADS, GLA_DK, GLA_DV), jnp.float32)
    y_prompt, p_conv_a, p_conv_b, p_lru_h, p_gla = trunk(x_prompt, z_a, z_b, z_h, z_s, p)
    y_sample, s_conv_a, s_conv_b, s_lru_h, s_gla = trunk(x_sample, state_conv_a, state_conv_b,
                                                        state_lru_h, state_gla, p)
    return (y_prompt, y_sample, p_conv_a, p_conv_b, p_lru_h, p_gla,
            s_conv_a, s_conv_b, s_lru_h, s_gla)
```

```python
import functools

import jax
import jax.numpy as jnp
from jax import lax
from jax.experimental import pallas as pl
from jax.experimental.pallas import tpu as pltpu

F32 = jnp.float32
BF16 = jnp.bfloat16
I32 = jnp.int32

EPS = 1e-6
LRU_C = 8.0
GLA_GATE_NORM = 16.0
TOP_K = 2
LANES = 128
NEG = -1e30
VMEM_LIMIT = 56 * 1024 * 1024


def _cp(sem, vmem=VMEM_LIMIT):
    return pltpu.CompilerParams(dimension_semantics=sem, vmem_limit_bytes=vmem)


def _tile(n, cands):
    for c in cands:
        if n % c == 0:
            return c
    return n


def _sigmoid(x):
    return 1.0 / (1.0 + jnp.exp(-x))


def _softplus(x):
    return jnp.maximum(x, 0.0) + jnp.log(1.0 + jnp.exp(-jnp.abs(x)))


def _log_sigmoid(x):
    return -_softplus(-x)


def _expm1(x):
    u = jnp.exp(x)
    near = jnp.abs(x) < 0.5
    safe_u = jnp.where(near & (u != 1.0), u, 2.0)
    return jnp.where(near, jnp.where(u == 1.0, x, (u - 1.0) * x / jnp.log(safe_u)), u - 1.0)


def _gelu_tanh(x):
    c = 0.7978845608028654
    return 0.5 * x * (1.0 + jnp.tanh(c * (x + 0.044715 * x * x * x)))


def _rms(x, g):
    return x * lax.rsqrt(jnp.mean(x * x, axis=-1, keepdims=True) + EPS) * g


def _norm_kernel(x_ref, g_ref, xn_ref):
    xn_ref[...] = _rms(x_ref[...], g_ref[...]).astype(xn_ref.dtype)


def _combine(x_ref, o0_ref, o1_ref, w_ref):
    w = w_ref[...]
    return x_ref[...] + (o0_ref[...] * w[:, 0:1] + o1_ref[...] * w[:, 1:2])


def _combine_norm_kernel(x_ref, o0_ref, o1_ref, w_ref, g_ref, xnew_ref, xn_ref):
    x = _combine(x_ref, o0_ref, o1_ref, w_ref)
    xnew_ref[...] = x
    xn_ref[...] = _rms(x, g_ref[...]).astype(xn_ref.dtype)


def _combine_final_kernel(x_ref, o0_ref, o1_ref, w_ref, g_ref, y_ref):
    y_ref[...] = _rms(_combine(x_ref, o0_ref, o1_ref, w_ref), g_ref[...])


def _router_kernel(x_ref, g_ref, wr_ref, br_ref, xn_ref, eid_ref, wts_ref, *, n_groups, per_group):
    xn = _rms(x_ref[...], g_ref[...])
    xn_ref[...] = xn
    logits = jnp.dot(xn, wr_ref[...], precision=lax.Precision.HIGHEST,
                     preferred_element_type=F32) + br_ref[...]
    tm = logits.shape[0]
    n_exp = n_groups * per_group
    lane = lax.broadcasted_iota(I32, (tm, LANES), 1)
    is_g = lane < n_groups
    lg = jnp.where(is_g, logits, NEG)
    mg = jnp.max(lg, axis=1, keepdims=True)
    gidx = jnp.min(jnp.where(is_g & (lg == mg), lane, LANES), axis=1, keepdims=True)
    pg = 1.0 / jnp.sum(jnp.where(is_g, jnp.exp(lg - mg), 0.0), axis=1, keepdims=True)
    lo = n_groups + gidx * per_group
    in_grp = (lane >= lo) & (lane < lo + per_group) & (lane < n_groups + n_exp)
    le = jnp.where(in_grp, logits, NEG)
    m1 = jnp.max(le, axis=1, keepdims=True)
    i1 = jnp.min(jnp.where(in_grp & (le == m1), lane, LANES), axis=1, keepdims=True)
    rest = in_grp & (lane != i1)
    le2 = jnp.where(rest, logits, NEG)
    m2 = jnp.max(le2, axis=1, keepdims=True)
    i2 = jnp.min(jnp.where(rest & (le2 == m2), lane, LANES), axis=1, keepdims=True)
    t = jnp.exp(m2 - m1)
    w1 = pg / (1.0 + t)
    w2 = pg * t / (1.0 + t)
    eid_ref[...] = jnp.where(lane == 0, i1 - n_groups, jnp.where(lane == 1, i2 - n_groups, 0))
    wts_ref[...] = jnp.where(lane == 0, w1, jnp.where(lane == 1, w2, 0.0))


def _row_spec(tm, d, off=0):
    return pl.BlockSpec((tm, d), lambda i: (i + off, 0))


def _const_spec(shape):
    return pl.BlockSpec(shape, lambda *_: tuple(0 for _ in shape))


def rmsnorm_rows(x, g, out_dtype):
    n, d = x.shape
    tm = _tile(n, (128, 64, 32, 16, 8))
    return pl.pallas_call(
        _norm_kernel, grid=(n // tm,),
        in_specs=[_row_spec(tm, d), _const_spec((1, d))],
        out_specs=_row_spec(tm, d),
        out_shape=jax.ShapeDtypeStruct((n, d), out_dtype),
        compiler_params=_cp(("parallel",)), name="rmsnorm",
    )(x, g.reshape(1, d))


def combine_norm(x, o, w, g):
    n, d = x.shape
    tm = _tile(n, (128, 64, 32, 16, 8))
    nb = n // tm
    return pl.pallas_call(
        _combine_norm_kernel, grid=(nb,),
        in_specs=[_row_spec(tm, d), _row_spec(tm, d), _row_spec(tm, d, nb), _row_spec(tm, LANES),
                  _const_spec((1, d))],
        out_specs=[_row_spec(tm, d), _row_spec(tm, d)],
        out_shape=[jax.ShapeDtypeStruct((n, d), F32), jax.ShapeDtypeStruct((n, d), BF16)],
        compiler_params=_cp(("parallel",)), name="combine_norm",
    )(x, o, o, w, g.reshape(1, d))


def combine_final(x, o, w, g, row0, rows):
    n, d = x.shape
    tm = next(c for c in (128, 64, 32, 16, 8) if rows % c == 0 and row0 % c == 0 and n % c == 0)
    off = row0 // tm
    return pl.pallas_call(
        _combine_final_kernel, grid=(rows // tm,),
        in_specs=[_row_spec(tm, d, off), _row_spec(tm, d, off), _row_spec(tm, d, off + n // tm),
                  _row_spec(tm, LANES, off), _const_spec((1, d))],
        out_specs=_row_spec(tm, d),
        out_shape=jax.ShapeDtypeStruct((rows, d), F32),
        compiler_params=_cp(("parallel",)), name="combine_final",
    )(x, o, o, w, g.reshape(1, d))


def router(x, g, wr, br, n_groups, per_group):
    n, d = x.shape
    tm = _tile(n, (128, 64, 32, 16, 8))
    kern = functools.partial(_router_kernel, n_groups=n_groups, per_group=per_group)
    return pl.pallas_call(
        kern, grid=(n // tm,),
        in_specs=[_row_spec(tm, d), _const_spec((1, d)), _const_spec((d, LANES)), _const_spec((1, LANES))],
        out_specs=[_row_spec(tm, d), _row_spec(tm, LANES), _row_spec(tm, LANES)],
        out_shape=[jax.ShapeDtypeStruct((n, d), F32), jax.ShapeDtypeStruct((n, LANES), I32),
                   jax.ShapeDtypeStruct((n, LANES), F32)],
        compiler_params=_cp(("parallel",)), name="router",
    )(x, g.reshape(1, d), wr, br)


def _mm_kernel(a_ref, b_ref, o_ref):
    o_ref[...] = jnp.dot(a_ref[...], b_ref[...], preferred_element_type=F32).astype(o_ref.dtype)


def _mm_res_kernel(a_ref, b_ref, r_ref, o_ref):
    o_ref[...] = r_ref[...] + jnp.dot(a_ref[...], b_ref[...], preferred_element_type=F32)


MM_TM = (1024, 832, 640, 512, 256, 128, 64, 32, 16)
MM_TN = (1024, 512, 256, 128)


def matmul(a, b, out_dtype=F32):
    m, k = a.shape
    _, n = b.shape
    tm, tn = _tile(m, MM_TM), _tile(n, MM_TN)
    return pl.pallas_call(
        _mm_kernel, grid=(m // tm, n // tn),
        in_specs=[pl.BlockSpec((tm, k), lambda i, j: (i, 0)), pl.BlockSpec((k, tn), lambda i, j: (0, j))],
        out_specs=pl.BlockSpec((tm, tn), lambda i, j: (i, j)),
        out_shape=jax.ShapeDtypeStruct((m, n), out_dtype),
        compiler_params=_cp(("parallel", "parallel")), name="matmul",
    )(a, b)


def matmul_add_rows(a, b, x, row0):
    m, k = a.shape
    _, n = b.shape
    tm, tn = _tile(m, MM_TM), _tile(n, MM_TN)
    assert row0 % tm == 0
    off = row0 // tm
    xs = pl.BlockSpec((tm, tn), lambda i, j: (i + off, j))
    return pl.pallas_call(
        _mm_res_kernel, grid=(m // tm, n // tn),
        in_specs=[pl.BlockSpec((tm, k), lambda i, j: (i, 0)), pl.BlockSpec((k, tn), lambda i, j: (0, j)), xs],
        out_specs=xs,
        out_shape=jax.ShapeDtypeStruct(x.shape, F32),
        input_output_aliases={2: 0},
        compiler_params=_cp(("parallel", "parallel")), name="matmul_add_rows",
    )(a, b, x)


def _lru_gates(xc, wai_ref, ba_ref, bx_ref, lam_ref, a_out, b_out):
    n_heads, hd, _ = wai_ref.shape
    for h in range(n_heads):
        sl = slice(h * hd, (h + 1) * hd)
        xh = xc[:, sl]
        ri = jnp.dot(xh.astype(BF16), wai_ref[h], preferred_element_type=F32)
        r = _sigmoid(ri[:, :hd] + ba_ref[:, sl])
        i = _sigmoid(ri[:, hd:] + bx_ref[:, sl])
        log_a = (-LRU_C) * r * _softplus(-lam_ref[:, sl])
        a_out[:, sl] = jnp.exp(log_a)
        b_out[:, sl] = jnp.sqrt(-_expm1(2.0 * log_a)) * (i * xh)


def _even_prompt_kernel(gb_ref, gc_ref, xa_ref, xr_ref, gr_ref, caw_ref, cbw_ref, cbb_ref, wai_ref,
                        ba_ref, bx_ref, lam_ref, y_ref, pa_ref, pb_ref, ph_ref,
                        ubuf, rbuf, hcar, abuf, bbuf, hbuf):
    t = pl.program_id(1)
    tt, w = gb_ref.shape
    ka, kb = caw_ref.shape[0], cbw_ref.shape[0]

    @pl.when(t == 0)
    def _():
        ubuf[0:8, :] = jnp.zeros((8, w), F32)
        rbuf[0:8, :] = jnp.zeros((8, w), F32)
        hcar[...] = jnp.zeros_like(hcar)

    u = gc_ref[...] * xa_ref[...]
    ubuf[8:8 + tt, :] = u
    v = caw_ref[ka - 1:ka, :] * u
    for k in range(1, ka):
        v = v + caw_ref[ka - 1 - k:ka - k, :] * ubuf[8 - k:8 - k + tt, :]
    y_ref[:, 0:w] = (gb_ref[...] * v).astype(y_ref.dtype)

    xr = xr_ref[...]
    rbuf[8:8 + tt, :] = xr
    xc = cbw_ref[kb - 1:kb, :] * xr + cbb_ref[...]
    for k in range(1, kb):
        xc = xc + cbw_ref[kb - 1 - k:kb - k, :] * rbuf[8 - k:8 - k + tt, :]
    _lru_gates(xc, wai_ref, ba_ref, bx_ref, lam_ref, abuf, bbuf)

    def step(i, h):
        h = abuf[pl.ds(i, 1), :] * h + bbuf[pl.ds(i, 1), :]
        hbuf[pl.ds(i, 1), :] = h
        return h

    h = lax.fori_loop(0, tt, step, hcar[0:1, :], unroll=8)
    hcar[0:1, :] = h
    y_ref[:, w:2 * w] = (_gelu_tanh(gr_ref[...]) * hbuf[...]).astype(y_ref.dtype)

    pa_ref[0] = ubuf[8 + tt - (ka - 1):8 + tt, :]
    pb_ref[0] = rbuf[8 + tt - (kb - 1):8 + tt, :]
    ph_ref[0] = h
    ubuf[0:8, :] = ubuf[tt:tt + 8, :]
    rbuf[0:8, :] = rbuf[tt:tt + 8, :]


def even_prompt(z, n_b, n_t, caw, cbw, cbb, wai, ba, bx, lam):
    n = n_b * n_t
    w = z.shape[1] // 5
    tt = _tile(n_t, (128, 64, 32, 16, 8))
    nt = n_t // tt
    ka, kb = caw.shape[0], cbw.shape[0]

    def zs(c):
        return pl.BlockSpec((tt, w), lambda b, t: (b * nt + t, c))

    return pl.pallas_call(
        _even_prompt_kernel, grid=(n_b, nt),
        in_specs=[zs(0), zs(1), zs(2), zs(3), zs(4), _const_spec((ka, w)), _const_spec((kb, w)),
                  _const_spec((1, w)), _const_spec(wai.shape), _const_spec((1, w)), _const_spec((1, w)),
                  _const_spec((1, w))],
        out_specs=[pl.BlockSpec((tt, 2 * w), lambda b, t: (b * nt + t, 0)),
                   pl.BlockSpec((1, ka - 1, w), lambda b, t: (b, 0, 0)),
                   pl.BlockSpec((1, kb - 1, w), lambda b, t: (b, 0, 0)),
                   pl.BlockSpec((1, 1, w), lambda b, t: (b, 0, 0))],
        out_shape=[jax.ShapeDtypeStruct((n, 2 * w), BF16),
                   jax.ShapeDtypeStruct((n_b, ka - 1, w), F32),
                   jax.ShapeDtypeStruct((n_b, kb - 1, w), F32),
                   jax.ShapeDtypeStruct((n_b, 1, w), F32)],
        scratch_shapes=[pltpu.VMEM((8 + tt, w), F32), pltpu.VMEM((8 + tt, w), F32), pltpu.VMEM((8, w), F32),
                        pltpu.VMEM((tt, w), F32), pltpu.VMEM((tt, w), F32), pltpu.VMEM((tt, w), F32)],
        compiler_params=_cp(("parallel", "arbitrary")), name="even_prompt",
    )(z, z, z, z, z, caw, cbw, cbb.reshape(1, w), wai, ba.reshape(1, w), bx.reshape(1, w), lam.reshape(1, w))


def _even_sample_kernel(gb_ref, gc_ref, xa_ref, xr_ref, gr_ref, sa_ref, sb_ref, sh_ref, caw_ref, cbw_ref,
                        cbb_ref, wai_ref, ba_ref, bx_ref, lam_ref, y_ref, na_ref, nb_ref, nh_ref,
                        abuf, bbuf):
    w = gb_ref.shape[1]
    ka, kb = caw_ref.shape[0], cbw_ref.shape[0]
    u = gc_ref[...] * xa_ref[...]
    v = caw_ref[ka - 1:ka, :] * u
    for k in range(1, ka):
        v = v + caw_ref[ka - 1 - k:ka - k, :] * sa_ref[:, (ka - 1 - k) * w:(ka - k) * w]
    y_ref[:, 0:w] = (gb_ref[...] * v).astype(y_ref.dtype)
    for k in range(ka - 2):
        na_ref[:, k * w:(k + 1) * w] = sa_ref[:, (k + 1) * w:(k + 2) * w]
    na_ref[:, (ka - 2) * w:(ka - 1) * w] = u

    xr = xr_ref[...]
    xc = cbw_ref[kb - 1:kb, :] * xr + cbb_ref[...]
    for k in range(1, kb):
        xc = xc + cbw_ref[kb - 1 - k:kb - k, :] * sb_ref[:, (kb - 1 - k) * w:(kb - k) * w]
    for k in range(kb - 2):
        nb_ref[:, k * w:(k + 1) * w] = sb_ref[:, (k + 1) * w:(k + 2) * w]
    nb_ref[:, (kb - 2) * w:(kb - 1) * w] = xr

    _lru_gates(xc, wai_ref, ba_ref, bx_ref, lam_ref, abuf, bbuf)
    h = abuf[...] * sh_ref[...] + bbuf[...]
    nh_ref[...] = h
    y_ref[:, w:2 * w] = (_gelu_tanh(gr_ref[...]) * h).astype(y_ref.dtype)


def even_sample(z, row0, sa, sb, sh, caw, cbw, cbb, wai, ba, bx, lam):
    s, w = sh.shape
    ka, kb = caw.shape[0], cbw.shape[0]
    assert row0 % s == 0
    rb = row0 // s

    def zs(c):
        return pl.BlockSpec((s, w), lambda i: (rb, c))

    full = lambda shape: pl.BlockSpec(shape, lambda i: tuple(0 for _ in shape))
    return pl.pallas_call(
        _even_sample_kernel, grid=(1,),
        in_specs=[zs(0), zs(1), zs(2), zs(3), zs(4), full(sa.shape), full(sb.shape), full(sh.shape),
                  full((ka, w)), full((kb, w)), full((1, w)), full(wai.shape), full((1, w)), full((1, w)),
                  full((1, w))],
        out_specs=[full((s, 2 * w)), full(sa.shape), full(sb.shape), full(sh.shape)],
        out_shape=[jax.ShapeDtypeStruct((s, 2 * w), BF16), jax.ShapeDtypeStruct(sa.shape, F32),
                   jax.ShapeDtypeStruct(sb.shape, F32), jax.ShapeDtypeStruct(sh.shape, F32)],
        scratch_shapes=[pltpu.VMEM((s, w), F32), pltpu.VMEM((s, w), F32)],
        compiler_params=_cp(("arbitrary",)), name="even_sample",
    )(z, z, z, z, z, sa, sb, sh, caw, cbw, cbb.reshape(1, w), wai, ba.reshape(1, w), bx.reshape(1, w),
      lam.reshape(1, w))


def _gla_log_decay(gr_ref, wa2_ref, ba_ref):
    pre = jnp.dot(gr_ref[...].astype(BF16), wa2_ref[...], preferred_element_type=F32) + ba_ref[...]
    return _log_sigmoid(pre) * (1.0 / GLA_GATE_NORM)


def _head_norm_gate(o, gn_ref, g):
    o = o * lax.rsqrt(jnp.mean(o * o, axis=-1, keepdims=True) + EPS) * gn_ref[...]
    return o * (g * _sigmoid(g))


def _gla_prompt_kernel(q_ref, k_ref, v_ref, g_ref, gr_ref, wa2_ref, ba_ref, gn_ref, o_ref, s_ref, st):
    c = pl.program_id(2)
    cs, dk = q_ref.shape

    @pl.when(c == 0)
    def _():
        st[...] = jnp.zeros_like(st)

    ga = _gla_log_decay(gr_ref, wa2_ref, ba_ref)
    row = lax.broadcasted_iota(I32, (cs, cs), 0)
    col = lax.broadcasted_iota(I32, (cs, cs), 1)
    causal = row >= col
    b = jnp.dot(causal.astype(F32), ga, precision=lax.Precision.HIGHEST, preferred_element_type=F32)
    b_last = b[cs - 1:cs, :]
    k = k_ref[...]
    qe = (q_ref[...] * (dk ** -0.5) * jnp.exp(b)).astype(BF16)
    ke = (k * jnp.exp(-b)).astype(BF16)
    kd = (k * jnp.exp(b_last - b)).astype(BF16)
    v16 = v_ref[...].astype(BF16)
    att = lax.dot_general(qe, ke, (((1,), (1,)), ((), ())), preferred_element_type=F32)
    att = jnp.where(causal, att, 0.0).astype(BF16)
    s_old = st[...]
    o = lax.dot_general(qe, s_old.astype(BF16), (((1,), (1,)), ((), ())), preferred_element_type=F32)
    o = o + jnp.dot(att, v16, preferred_element_type=F32)
    st[...] = s_old * jnp.exp(b_last) + lax.dot_general(v16, kd, (((0,), (0,)), ((), ())),
                                                        preferred_element_type=F32)
    o_ref[...] = _head_norm_gate(o, gn_ref, g_ref[...]).astype(o_ref.dtype)

    @pl.when(c == pl.num_programs(2) - 1)
    def _():
        s_ref[0, 0] = st[...].T


def gla_prompt(z, zg, n_b, n_t, n_h, dk, dv, wa2p, ba, gnorm):
    n = n_b * n_t
    cs = _tile(n_t, (64, 32, 16, 8))
    nc = n_t // cs
    kb0 = n_h
    vb0 = (2 * n_h * dk) // dv
    gb0 = vb0 + n_h
    assert (2 * n_h * dk) % dv == 0
    row = lambda b, h, c: b * nc + c
    return pl.pallas_call(
        _gla_prompt_kernel, grid=(n_b, n_h, nc),
        in_specs=[pl.BlockSpec((cs, dk), lambda b, h, c: (row(b, h, c), h)),
                  pl.BlockSpec((cs, dk), lambda b, h, c: (row(b, h, c), kb0 + h)),
                  pl.BlockSpec((cs, dv), lambda b, h, c: (row(b, h, c), vb0 + h)),
                  pl.BlockSpec((cs, dv), lambda b, h, c: (row(b, h, c), gb0 + h)),
                  pl.BlockSpec((cs, LANES), lambda b, h, c: (row(b, h, c), 0)),
                  pl.BlockSpec((LANES, dk), lambda b, h, c: (0, h)),
                  pl.BlockSpec((1, dk), lambda b, h, c: (0, h)),
                  pl.BlockSpec((1, dv), lambda b, h, c: (0, 0))],
        out_specs=[pl.BlockSpec((cs, dv), lambda b, h, c: (row(b, h, c), h)),
                   pl.BlockSpec((1, 1, dk, dv), lambda b, h, c: (b, h, 0, 0))],
        out_shape=[jax.ShapeDtypeStruct((n, n_h * dv), BF16),
                   jax.ShapeDtypeStruct((n_b, n_h, dk, dv), F32)],
        scratch_shapes=[pltpu.VMEM((dv, dk), F32)],
        compiler_params=_cp(("parallel", "parallel", "arbitrary")), name="gla_prompt",
    )(z, z, z, z, zg, wa2p, ba.reshape(1, -1), gnorm.reshape(1, dv))


def _gla_sample_kernel(q_ref, k_ref, v_ref, g_ref, gr_ref, wa2_ref, ba_ref, gn_ref, s0_ref,
                       o_ref, s_ref, ta, tk, tq):
    i = pl.program_id(1)
    nblk, dk, bb = ta.shape

    @pl.when(i == 0)
    def _():
        alpha_t = jnp.exp(_gla_log_decay(gr_ref, wa2_ref, ba_ref)).T
        k_t = k_ref[...].T
        q_t = (q_ref[...] * (dk ** -0.5)).T
        for j in range(nblk):
            ta[j] = alpha_t[:, j * bb:(j + 1) * bb]
            tk[j] = k_t[:, j * bb:(j + 1) * bb]
            tq[j] = q_t[:, j * bb:(j + 1) * bb]

    a3, k3, q3 = ta[i], tk[i], tq[i]
    v = v_ref[...]
    rows = []
    for j in range(bb):
        s_new = s0_ref[j, 0] * a3[:, j:j + 1] + k3[:, j:j + 1] * v[j:j + 1, :]
        s_ref[j, 0] = s_new
        rows.append(jnp.sum(q3[:, j:j + 1] * s_new, axis=0, keepdims=True))
    o = jnp.concatenate(rows, axis=0)
    o_ref[...] = _head_norm_gate(o, gn_ref, g_ref[...]).astype(o_ref.dtype)


def gla_sample(z, zg, row0, s0, wa2p, ba, gnorm):
    s, n_h, dk, dv = s0.shape
    bb = _tile(s, (16, 8))
    nblk = s // bb
    assert row0 % s == 0 and (2 * n_h * dk) % dv == 0
    rs, rb = row0 // s, row0 // bb
    kb0 = n_h
    vb0 = (2 * n_h * dk) // dv
    gb0 = vb0 + n_h
    return pl.pallas_call(
        _gla_sample_kernel, grid=(n_h, nblk),
        in_specs=[pl.BlockSpec((s, dk), lambda h, i: (rs, h)),
                  pl.BlockSpec((s, dk), lambda h, i: (rs, kb0 + h)),
                  pl.BlockSpec((bb, dv), lambda h, i: (rb + i, vb0 + h)),
                  pl.BlockSpec((bb, dv), lambda h, i: (rb + i, gb0 + h)),
                  pl.BlockSpec((s, LANES), lambda h, i: (rs, 0)),
                  pl.BlockSpec((LANES, dk), lambda h, i: (0, h)),
                  pl.BlockSpec((1, dk), lambda h, i: (0, h)),
                  pl.BlockSpec((1, dv), lambda h, i: (0, 0)),
                  pl.BlockSpec((bb, 1, dk, dv), lambda h, i: (i, h, 0, 0))],
        out_specs=[pl.BlockSpec((bb, dv), lambda h, i: (i, h)),
                   pl.BlockSpec((bb, 1, dk, dv), lambda h, i: (i, h, 0, 0))],
        out_shape=[jax.ShapeDtypeStruct((s, n_h * dv), BF16), jax.ShapeDtypeStruct(s0.shape, F32)],
        scratch_shapes=[pltpu.VMEM((nblk, dk, bb), F32)] * 3,
        compiler_params=_cp(("arbitrary", "arbitrary")), name="gla_sample",
    )(z, z, z, z, zg, wa2p, ba.reshape(1, -1), gnorm.reshape(1, dv), s0)


MOE_ROWS = 512
MOE_TM = 256


def _moe_kernel(ie_ref, icnt_ref, tok_ref, dst_ref, xn_hbm, wg_ref, wu_ref, wd_ref, o_hbm,
                fbuf, xb16, hbuf, wg16, wu16, wd16, sem_in, sem_out, *, nf, nd, tm):
    del ie_ref
    it = pl.program_id(0)
    s = pl.program_id(1)
    cnt = icnt_ref[it]
    ntile = (cnt + tm - 1) // tm
    fc = wg16.shape[1]
    dc = wd16.shape[1]

    def gather_copy(r):
        return pltpu.make_async_copy(xn_hbm.at[pl.ds(tok_ref[0, 0, r], 1), :], fbuf.at[pl.ds(r, 1), :], sem_in)

    def scatter_copy(r):
        return pltpu.make_async_copy(fbuf.at[pl.ds(r, 1), :], o_hbm.at[pl.ds(dst_ref[0, 0, r], 1), :], sem_out)

    def for_rows(fn):
        def body(r, c):
            fn(r)
            return c
        lax.fori_loop(0, cnt, body, 0)

    def for_tiles(fn):
        def body(j, c):
            fn(pl.multiple_of(j * tm, tm))
            return c
        lax.fori_loop(0, ntile, body, 0)

    @pl.when((it == 0) & (s == 0))
    def _():
        fbuf[...] = jnp.zeros_like(fbuf)

    @pl.when((s == 0) & (cnt > 0))
    def _():
        for_rows(lambda r: gather_copy(r).start())
        for_rows(lambda r: gather_copy(r).wait())

        def cast(r0):
            xb16[pl.ds(r0, tm), :] = fbuf[pl.ds(r0, tm), :].astype(BF16)
        for_tiles(cast)

    for f in range(nf):
        @pl.when((s == f) & (cnt > 0))
        def _(f=f):
            wg16[...] = wg_ref[0].astype(BF16)
            wu16[...] = wu_ref[0].astype(BF16)

            def up(r0):
                x = xb16[pl.ds(r0, tm), :]
                g = jnp.dot(x, wg16[...], preferred_element_type=F32)
                u = jnp.dot(x, wu16[...], preferred_element_type=F32)
                hbuf[pl.ds(r0, tm), f * fc:(f + 1) * fc] = (g * _sigmoid(g) * u).astype(BF16)
            for_tiles(up)

    for d in range(nd):
        @pl.when((s == nf + d) & (cnt > 0))
        def _(d=d):
            wd16[...] = wd_ref[0].astype(BF16)

            def down(r0):
                fbuf[pl.ds(r0, tm), d * dc:(d + 1) * dc] = jnp.dot(hbuf[pl.ds(r0, tm), :], wd16[...],
                                                                 preferred_element_type=F32)
            for_tiles(down)

    @pl.when((s == nf + nd - 1) & (cnt > 0))
    def _():
        for_rows(lambda r: scatter_copy(r).start())
        for_rows(lambda r: scatter_copy(r).wait())


def moe_route_tables(eid, n_exp, rows):
    n = eid.shape[0]
    flat = eid.reshape(-1)
    na = flat.shape[0]
    order = jnp.argsort(flat).astype(I32)
    counts = jnp.bincount(flat, length=n_exp).astype(I32)
    gstart = jnp.cumsum(counts) - counts
    n_it = (counts + rows - 1) // rows
    it_cum = jnp.cumsum(n_it)
    it_base = it_cum - n_it
    total = it_cum[-1]
    n_items = n_exp + na // rows
    ii = jnp.arange(n_items, dtype=I32)
    e_i = jnp.minimum(jnp.searchsorted(it_cum, ii, side="right").astype(I32), n_exp - 1)
    valid = ii < total
    e_i = jnp.where(valid, e_i, e_i[jnp.maximum(total - 1, 0)])
    j = ii - it_base[e_i]
    start = gstart[e_i] + j * rows
    cnt = jnp.where(valid, jnp.minimum(rows, counts[e_i] - j * rows), 0).astype(I32)
    idx = jnp.clip(start[:, None] + jnp.arange(rows, dtype=I32)[None, :], 0, na - 1)
    a = order[idx]
    tok = a // TOP_K
    dst = (a % TOP_K) * n + tok
    return e_i, cnt, tok.reshape(n_items, 1, rows), dst.reshape(n_items, 1, rows)


def moe_ffn(xn, eid, wg, wu, wd):
    n, d = xn.shape
    n_exp, _, f = wg.shape
    rows, tm = MOE_ROWS, MOE_TM
    fc = _tile(f, (256, 128))
    dc = _tile(d, (1024, 512, 256, 128))
    nf, nd = f // fc, d // dc
    e_i, cnt, tok, dst = moe_route_tables(eid, n_exp, rows)
    n_items = e_i.shape[0]
    kern = functools.partial(_moe_kernel, nf=nf, nd=nd, tm=tm)
    grid_spec = pltpu.PrefetchScalarGridSpec(
        num_scalar_prefetch=2, grid=(n_items, nf + nd),
        in_specs=[pl.BlockSpec((1, 1, rows), lambda i, s, ie, ic: (i, 0, 0), memory_space=pltpu.SMEM),
                  pl.BlockSpec((1, 1, rows), lambda i, s, ie, ic: (i, 0, 0), memory_space=pltpu.SMEM),
                  pl.BlockSpec(memory_space=pl.ANY),
                  pl.BlockSpec((1, d, fc), lambda i, s, ie, ic: (ie[i], 0, jnp.minimum(s, nf - 1))),
                  pl.BlockSpec((1, d, fc), lambda i, s, ie, ic: (ie[i], 0, jnp.minimum(s, nf - 1))),
                  pl.BlockSpec((1, f, dc), lambda i, s, ie, ic: (ie[i], 0, jnp.maximum(s - nf, 0)))],
        out_specs=pl.BlockSpec(memory_space=pl.ANY),
        scratch_shapes=[pltpu.VMEM((rows, d), F32), pltpu.VMEM((rows, d), BF16), pltpu.VMEM((rows, f), BF16),
                        pltpu.VMEM((d, fc), BF16), pltpu.VMEM((d, fc), BF16), pltpu.VMEM((f, dc), BF16),
                        pltpu.SemaphoreType.DMA(()), pltpu.SemaphoreType.DMA(())])
    return pl.pallas_call(
        kern, grid_spec=grid_spec,
        out_shape=jax.ShapeDtypeStruct((TOP_K * n, d), F32),
        compiler_params=_cp(("arbitrary", "arbitrary")), name="moe_ffn",
    )(e_i, cnt, tok, dst, xn, wg, wu, wd)


def moe_layer(x, g, wrg, brg, wre, bre, wg, wu, wd):
    n_groups = wrg.shape[1]
    n_exp = wre.shape[1]
    d = x.shape[1]
    wr = jnp.zeros((d, LANES), F32).at[:, :n_groups].set(wrg).at[:, n_groups:n_groups + n_exp].set(wre)
    br = jnp.zeros((1, LANES), F32).at[0, :n_groups].set(brg).at[0, n_groups:n_groups + n_exp].set(bre)
    xn, eid, wts = router(x, g, wr, br, n_groups, n_exp // n_groups)
    o = moe_ffn(xn, eid[:, :TOP_K], wg, wu, wd)
    return o, wts


def kernel(x_prompt, x_sample, state_conv_a, state_conv_b, state_lru_h, state_gla, norm_mix, norm_ffn,
           norm_final, w_in_even, conv_a_w, conv_b_w, conv_b_b, lru_wa, lru_ba, lru_wx, lru_bx, lru_lambda,
           w_out_even, w_in_odd, gla_wa2, gla_ba, gla_norm, w_out_odd, router_group_w, router_group_b,
           router_expert_w, router_expert_b, moe_w_gate, moe_w_up, moe_w_down):
    pb, pt, d = x_prompt.shape
    sb, st_, _ = x_sample.shape
    assert st_ == 1
    n_p = pb * pt
    n = n_p + sb
    depth = norm_mix.shape[0]
    w = state_lru_h.shape[-1]
    n_h, dk, dv = state_gla.shape[2:]

    x = jnp.concatenate([x_prompt.reshape(n_p, d), x_sample.reshape(sb, d)], axis=0)
    new = {k: [] for k in ("pa", "pb", "ph", "ps", "sa", "sb", "sh", "ss")}
    pending = None

    for l in range(depth):
        e = l // 2
        if pending is None:
            xn = rmsnorm_rows(x, norm_mix[l], BF16)
        else:
            x, xn = combine_norm(x, pending[0], pending[1], norm_mix[l])
        if l % 2 == 0:
            z = matmul(xn, w_in_even[e].astype(BF16))
            wai = jnp.concatenate([lru_wa[e], lru_wx[e]], axis=-1).astype(BF16)
            mix = (conv_a_w[e], conv_b_w[e], conv_b_b[e], wai, lru_ba[e], lru_bx[e], lru_lambda[e])
            y_pr, p_a, p_b, p_h = even_prompt(z, pb, pt, *mix)
            ka, kb = conv_a_w.shape[1], conv_b_w.shape[1]
            y_sm, s_a, s_b, s_h = even_sample(z, n_p, state_conv_a[e].reshape(sb, (ka - 1) * w),
                                              state_conv_b[e].reshape(sb, (kb - 1) * w), state_lru_h[e], *mix)
            new["pa"].append(p_a)
            new["pb"].append(p_b)
            new["ph"].append(p_h.reshape(pb, w))
            new["sa"].append(s_a.reshape(sb, ka - 1, w))
            new["sb"].append(s_b.reshape(sb, kb - 1, w))
            new["sh"].append(s_h)
            w_out = w_out_even[e].astype(BF16)
        else:
            n_main = 2 * n_h * dk + 2 * n_h * dv
            rank = w_in_odd.shape[2] - n_main
            z = matmul(xn, w_in_odd[e][:, :n_main].astype(BF16))
            w_gate = jnp.zeros((d, LANES), BF16).at[:, :rank].set(w_in_odd[e][:, n_main:].astype(BF16))
            zg = matmul(xn, w_gate)
            wa2p = jnp.zeros((LANES, n_h * dk), BF16).at[:rank].set(gla_wa2[e].astype(BF16))
            y_pr, p_s = gla_prompt(z, zg, pb, pt, n_h, dk, dv, wa2p, gla_ba[e], gla_norm[e])
            y_sm, s_s = gla_sample(z, zg, n_p, state_gla[e], wa2p, gla_ba[e], gla_norm[e])
            new["ps"].append(p_s)
            new["ss"].append(s_s)
            w_out = w_out_odd[e].astype(BF16)
        x = matmul_add_rows(y_pr, w_out, x, 0)
        x = matmul_add_rows(y_sm, w_out, x, n_p)
        pending = moe_layer(x, norm_ffn[l], router_group_w[l], router_group_b[l], router_expert_w[l],
                            router_expert_b[l], moe_w_gate[l], moe_w_up[l], moe_w_down[l])

    y_p = combine_final(x, pending[0], pending[1], norm_final, 0, n_p).reshape(pb, pt, d)
    y_s = combine_final(x, pending[0], pending[1], norm_final, n_p, sb).reshape(sb, 1, d)
    stack = lambda k: jnp.stack(new[k])
    return (y_p, y_s, stack("pa"), stack("pb"), stack("ph"), stack("ps"),
            stack("sa"), stack("sb"), stack("sh"), stack("ss"))
```

```python
import functools

import jax
import jax.numpy as jnp
from jax import lax
from jax.experimental import pallas as pl
from jax.experimental.pallas import tpu as pltpu

F32 = jnp.float32
BF16 = jnp.bfloat16
I32 = jnp.int32

EPS = 1e-6
LRU_C = 8.0
GLA_GATE_NORM = 16.0
TOP_K = 2
LANES = 128
NEG = -1e30
VMEM_LIMIT = 56 * 1024 * 1024


def _cp(sem, vmem=VMEM_LIMIT):
    return pltpu.CompilerParams(dimension_semantics=sem, vmem_limit_bytes=vmem)


def _tile(n, cands):
    for c in cands:
        if n % c == 0:
            return c
    return n


def _sigmoid(x):
    return 1.0 / (1.0 + jnp.exp(-x))


def _softplus(x):
    return jnp.maximum(x, 0.0) + jnp.log(1.0 + jnp.exp(-jnp.abs(x)))


def _log_sigmoid(x):
    return -_softplus(-x)


def _expm1(x):
    u = jnp.exp(x)
    near = jnp.abs(x) < 0.5
    safe_u = jnp.where(near & (u != 1.0), u, 2.0)
    return jnp.where(near, jnp.where(u == 1.0, x, (u - 1.0) * x / jnp.log(safe_u)), u - 1.0)


def _gelu_tanh(x):
    c = 0.7978845608028654
    return 0.5 * x * (1.0 + jnp.tanh(c * (x + 0.044715 * x * x * x)))


def _rms(x, g):
    return x * lax.rsqrt(jnp.mean(x * x, axis=-1, keepdims=True) + EPS) * g


def _norm_kernel(x_ref, g_ref, xn_ref):
    xn_ref[...] = _rms(x_ref[...], g_ref[...]).astype(xn_ref.dtype)


def _combine(x_ref, o0_ref, o1_ref, w_ref):
    w = w_ref[...]
    return x_ref[...] + (o0_ref[...] * w[:, 0:1] + o1_ref[...] * w[:, 1:2])


def _combine_norm_kernel(x_ref, o0_ref, o1_ref, w_ref, g_ref, xnew_ref, xn_ref):
    x = _combine(x_ref, o0_ref, o1_ref, w_ref)
    xnew_ref[...] = x
    xn_ref[...] = _rms(x, g_ref[...]).astype(xn_ref.dtype)


def _combine_final_kernel(x_ref, o0_ref, o1_ref, w_ref, g_ref, y_ref):
    y_ref[...] = _rms(_combine(x_ref, o0_ref, o1_ref, w_ref), g_ref[...])


def _router_kernel(x_ref, g_ref, wr_ref, br_ref, xn_ref, eid_ref, wts_ref, *, n_groups, per_group):
    xn = _rms(x_ref[...], g_ref[...])
    xn_ref[...] = xn
    logits = jnp.dot(xn.astype(BF16), wr_ref[...], preferred_element_type=F32) + br_ref[...]
    tm = logits.shape[0]
    n_exp = n_groups * per_group
    lane = lax.broadcasted_iota(I32, (tm, LANES), 1)
    is_g = lane < n_groups
    lg = jnp.where(is_g, logits, NEG)
    mg = jnp.max(lg, axis=1, keepdims=True)
    gidx = jnp.min(jnp.where(is_g & (lg == mg), lane, LANES), axis=1, keepdims=True)
    pg = 1.0 / jnp.sum(jnp.where(is_g, jnp.exp(lg - mg), 0.0), axis=1, keepdims=True)
    lo = n_groups + gidx * per_group
    in_grp = (lane >= lo) & (lane < lo + per_group) & (lane < n_groups + n_exp)
    le = jnp.where(in_grp, logits, NEG)
    m1 = jnp.max(le, axis=1, keepdims=True)
    i1 = jnp.min(jnp.where(in_grp & (le == m1), lane, LANES), axis=1, keepdims=True)
    rest = in_grp & (lane != i1)
    le2 = jnp.where(rest, logits, NEG)
    m2 = jnp.max(le2, axis=1, keepdims=True)
    i2 = jnp.min(jnp.where(rest & (le2 == m2), lane, LANES), axis=1, keepdims=True)
    t = jnp.exp(m2 - m1)
    w1 = pg / (1.0 + t)
    w2 = pg * t / (1.0 + t)
    eid_ref[...] = jnp.where(lane == 0, i1 - n_groups, jnp.where(lane == 1, i2 - n_groups, 0))
    wts_ref[...] = jnp.where(lane == 0, w1, jnp.where(lane == 1, w2, 0.0))


def _row_spec(tm, d, off=0):
    return pl.BlockSpec((tm, d), lambda i: (i + off, 0))


def _const_spec(shape):
    return pl.BlockSpec(shape, lambda *_: tuple(0 for _ in shape))


def rmsnorm_rows(x, g, out_dtype):
    n, d = x.shape
    tm = _tile(n, (128, 64, 32, 16, 8))
    return pl.pallas_call(
        _norm_kernel, grid=(n // tm,),
        in_specs=[_row_spec(tm, d), _const_spec((1, d))],
        out_specs=_row_spec(tm, d),
        out_shape=jax.ShapeDtypeStruct((n, d), out_dtype),
        compiler_params=_cp(("parallel",)), name="rmsnorm",
    )(x, g.reshape(1, d))


def combine_norm(x, o, w, g):
    n, d = x.shape
    tm = _tile(n, (128, 64, 32, 16, 8))
    nb = n // tm
    return pl.pallas_call(
        _combine_norm_kernel, grid=(nb,),
        in_specs=[_row_spec(tm, d), _row_spec(tm, d), _row_spec(tm, d, nb), _row_spec(tm, LANES),
                  _const_spec((1, d))],
        out_specs=[_row_spec(tm, d), _row_spec(tm, d)],
        out_shape=[jax.ShapeDtypeStruct((n, d), F32), jax.ShapeDtypeStruct((n, d), BF16)],
        compiler_params=_cp(("parallel",)), name="combine_norm",
    )(x, o, o, w, g.reshape(1, d))


def combine_final(x, o, w, g, row0, rows):
    n, d = x.shape
    tm = next(c for c in (128, 64, 32, 16, 8) if rows % c == 0 and row0 % c == 0 and n % c == 0)
    off = row0 // tm
    return pl.pallas_call(
        _combine_final_kernel, grid=(rows // tm,),
        in_specs=[_row_spec(tm, d, off), _row_spec(tm, d, off), _row_spec(tm, d, off + n // tm),
                  _row_spec(tm, LANES, off), _const_spec((1, d))],
        out_specs=_row_spec(tm, d),
        out_shape=jax.ShapeDtypeStruct((rows, d), F32),
        compiler_params=_cp(("parallel",)), name="combine_final",
    )(x, o, o, w, g.reshape(1, d))


def router(x, g, wr, br, n_groups, per_group):
    n, d = x.shape
    tm = _tile(n, (128, 64, 32, 16, 8))
    kern = functools.partial(_router_kernel, n_groups=n_groups, per_group=per_group)
    return pl.pallas_call(
        kern, grid=(n // tm,),
        in_specs=[_row_spec(tm, d), _const_spec((1, d)), _const_spec((d, LANES)), _const_spec((1, LANES))],
        out_specs=[_row_spec(tm, d), _row_spec(tm, LANES), _row_spec(tm, LANES)],
        out_shape=[jax.ShapeDtypeStruct((n, d), F32), jax.ShapeDtypeStruct((n, LANES), I32),
                   jax.ShapeDtypeStruct((n, LANES), F32)],
        compiler_params=_cp(("parallel",)), name="router",
    )(x, g.reshape(1, d), wr, br)


def _mm_kernel(a_ref, b_ref, o_ref):
    o_ref[...] = jnp.dot(a_ref[...], b_ref[...], preferred_element_type=F32).astype(o_ref.dtype)


def _mm_res_kernel(a_ref, b_ref, r_ref, o_ref):
    o_ref[...] = r_ref[...] + jnp.dot(a_ref[...], b_ref[...], preferred_element_type=F32)


MM_TM = (1024, 832, 640, 512, 256, 128, 64, 32, 16)
MM_TN = (1024, 512, 256, 128)


def matmul(a, b, out_dtype=F32):
    m, k = a.shape
    _, n = b.shape
    tm, tn = _tile(m, MM_TM), _tile(n, MM_TN)
    return pl.pallas_call(
        _mm_kernel, grid=(m // tm, n // tn),
        in_specs=[pl.BlockSpec((tm, k), lambda i, j: (i, 0)), pl.BlockSpec((k, tn), lambda i, j: (0, j))],
        out_specs=pl.BlockSpec((tm, tn), lambda i, j: (i, j)),
        out_shape=jax.ShapeDtypeStruct((m, n), out_dtype),
        compiler_params=_cp(("parallel", "parallel")), name="matmul",
    )(a, b)


def matmul_add_rows(a, b, x, row0):
    m, k = a.shape
    _, n = b.shape
    tm, tn = _tile(m, MM_TM), _tile(n, MM_TN)
    assert row0 % tm == 0
    off = row0 // tm
    xs = pl.BlockSpec((tm, tn), lambda i, j: (i + off, j))
    return pl.pallas_call(
        _mm_res_kernel, grid=(m // tm, n // tn),
        in_specs=[pl.BlockSpec((tm, k), lambda i, j: (i, 0)), pl.BlockSpec((k, tn), lambda i, j: (0, j)), xs],
        out_specs=xs,
        out_shape=jax.ShapeDtypeStruct(x.shape, F32),
        input_output_aliases={2: 0},
        compiler_params=_cp(("parallel", "parallel")), name="matmul_add_rows",
    )(a, b, x)


def _lru_gates(xc, wai_ref, ba_ref, bx_ref, lam_ref, a_out, b_out):
    n_heads, hd, _ = wai_ref.shape
    for h in range(n_heads):
        sl = slice(h * hd, (h + 1) * hd)
        xh = xc[:, sl]
        ri = jnp.dot(xh.astype(BF16), wai_ref[h], preferred_element_type=F32)
        r = _sigmoid(ri[:, :hd] + ba_ref[:, sl])
        i = _sigmoid(ri[:, hd:] + bx_ref[:, sl])
        log_a = (-LRU_C) * r * _softplus(-lam_ref[:, sl])
        a_out[:, sl] = jnp.exp(log_a)
        b_out[:, sl] = jnp.sqrt(-_expm1(2.0 * log_a)) * (i * xh)


def _even_prompt_kernel(gb_ref, gc_ref, xa_ref, xr_ref, gr_ref, caw_ref, cbw_ref, cbb_ref, wai_ref,
                        ba_ref, bx_ref, lam_ref, y_ref, pa_ref, pb_ref, ph_ref,
                        ubuf, rbuf, hcar, abuf, bbuf, hbuf):
    t = pl.program_id(1)
    tt, w = gb_ref.shape
    ka, kb = caw_ref.shape[0], cbw_ref.shape[0]

    @pl.when(t == 0)
    def _():
        ubuf[0:8, :] = jnp.zeros((8, w), F32)
        rbuf[0:8, :] = jnp.zeros((8, w), F32)
        hcar[...] = jnp.zeros_like(hcar)

    u = gc_ref[...] * xa_ref[...]
    ubuf[8:8 + tt, :] = u
    v = caw_ref[ka - 1:ka, :] * u
    for k in range(1, ka):
        v = v + caw_ref[ka - 1 - k:ka - k, :] * ubuf[8 - k:8 - k + tt, :]
    y_ref[:, 0:w] = (gb_ref[...] * v).astype(y_ref.dtype)

    xr = xr_ref[...]
    rbuf[8:8 + tt, :] = xr
    xc = cbw_ref[kb - 1:kb, :] * xr + cbb_ref[...]
    for k in range(1, kb):
        xc = xc + cbw_ref[kb - 1 - k:kb - k, :] * rbuf[8 - k:8 - k + tt, :]
    _lru_gates(xc, wai_ref, ba_ref, bx_ref, lam_ref, abuf, bbuf)

    def step(i, h):
        h = abuf[pl.ds(i, 1), :] * h + bbuf[pl.ds(i, 1), :]
        hbuf[pl.ds(i, 1), :] = h
        return h

    h = lax.fori_loop(0, tt, step, hcar[0:1, :], unroll=8)
    hcar[0:1, :] = h
    y_ref[:, w:2 * w] = (_gelu_tanh(gr_ref[...]) * hbuf[...]).astype(y_ref.dtype)

    pa_ref[0] = ubuf[8 + tt - (ka - 1):8 + tt, :]
    pb_ref[0] = rbuf[8 + tt - (kb - 1):8 + tt, :]
    ph_ref[0] = h
    ubuf[0:8, :] = ubuf[tt:tt + 8, :]
    rbuf[0:8, :] = rbuf[tt:tt + 8, :]


def even_prompt(z, n_b, n_t, caw, cbw, cbb, wai, ba, bx, lam):
    n = n_b * n_t
    w = z.shape[1] // 5
    tt = _tile(n_t, (128, 64, 32, 16, 8))
    nt = n_t // tt
    ka, kb = caw.shape[0], cbw.shape[0]

    def zs(c):
        return pl.BlockSpec((tt, w), lambda b, t: (b * nt + t, c))

    return pl.pallas_call(
        _even_prompt_kernel, grid=(n_b, nt),
        in_specs=[zs(0), zs(1), zs(2), zs(3), zs(4), _const_spec((ka, w)), _const_spec((kb, w)),
                  _const_spec((1, w)), _const_spec(wai.shape), _const_spec((1, w)), _const_spec((1, w)),
                  _const_spec((1, w))],
        out_specs=[pl.BlockSpec((tt, 2 * w), lambda b, t: (b * nt + t, 0)),
                   pl.BlockSpec((1, ka - 1, w), lambda b, t: (b, 0, 0)),
                   pl.BlockSpec((1, kb - 1, w), lambda b, t: (b, 0, 0)),
                   pl.BlockSpec((1, 1, w), lambda b, t: (b, 0, 0))],
        out_shape=[jax.ShapeDtypeStruct((n, 2 * w), BF16),
                   jax.ShapeDtypeStruct((n_b, ka - 1, w), F32),
                   jax.ShapeDtypeStruct((n_b, kb - 1, w), F32),
                   jax.ShapeDtypeStruct((n_b, 1, w), F32)],
        scratch_shapes=[pltpu.VMEM((8 + tt, w), F32), pltpu.VMEM((8 + tt, w), F32), pltpu.VMEM((8, w), F32),
                        pltpu.VMEM((tt, w), F32), pltpu.VMEM((tt, w), F32), pltpu.VMEM((tt, w), F32)],
        compiler_params=_cp(("parallel", "arbitrary")), name="even_prompt",
    )(z, z, z, z, z, caw, cbw, cbb.reshape(1, w), wai, ba.reshape(1, w), bx.reshape(1, w), lam.reshape(1, w))


def _even_sample_kernel(gb_ref, gc_ref, xa_ref, xr_ref, gr_ref, sa_ref, sb_ref, sh_ref, caw_ref, cbw_ref,
                        cbb_ref, wai_ref, ba_ref, bx_ref, lam_ref, y_ref, na_ref, nb_ref, nh_ref,
                        abuf, bbuf):
    w = gb_ref.shape[1]
    ka, kb = caw_ref.shape[0], cbw_ref.shape[0]
    u = gc_ref[...] * xa_ref[...]
    v = caw_ref[ka - 1:ka, :] * u
    for k in range(1, ka):
        v = v + caw_ref[ka - 1 - k:ka - k, :] * sa_ref[:, (ka - 1 - k) * w:(ka - k) * w]
    y_ref[:, 0:w] = (gb_ref[...] * v).astype(y_ref.dtype)
    for k in range(ka - 2):
        na_ref[:, k * w:(k + 1) * w] = sa_ref[:, (k + 1) * w:(k + 2) * w]
    na_ref[:, (ka - 2) * w:(ka - 1) * w] = u

    xr = xr_ref[...]
    xc = cbw_ref[kb - 1:kb, :] * xr + cbb_ref[...]
    for k in range(1, kb):
        xc = xc + cbw_ref[kb - 1 - k:kb - k, :] * sb_ref[:, (kb - 1 - k) * w:(kb - k) * w]
    for k in range(kb - 2):
        nb_ref[:, k * w:(k + 1) * w] = sb_ref[:, (k + 1) * w:(k + 2) * w]
    nb_ref[:, (kb - 2) * w:(kb - 1) * w] = xr

    _lru_gates(xc, wai_ref, ba_ref, bx_ref, lam_ref, abuf, bbuf)
    h = abuf[...] * sh_ref[...] + bbuf[...]
    nh_ref[...] = h
    y_ref[:, w:2 * w] = (_gelu_tanh(gr_ref[...]) * h).astype(y_ref.dtype)


def even_sample(z, row0, sa, sb, sh, caw, cbw, cbb, wai, ba, bx, lam):
    s, w = sh.shape
    ka, kb = caw.shape[0], cbw.shape[0]
    assert row0 % s == 0
    rb = row0 // s

    def zs(c):
        return pl.BlockSpec((s, w), lambda i: (rb, c))

    full = lambda shape: pl.BlockSpec(shape, lambda i: tuple(0 for _ in shape))
    return pl.pallas_call(
        _even_sample_kernel, grid=(1,),
        in_specs=[zs(0), zs(1), zs(2), zs(3), zs(4), full(sa.shape), full(sb.shape), full(sh.shape),
                  full((ka, w)), full((kb, w)), full((1, w)), full(wai.shape), full((1, w)), full((1, w)),
                  full((1, w))],
        out_specs=[full((s, 2 * w)), full(sa.shape), full(sb.shape), full(sh.shape)],
        out_shape=[jax.ShapeDtypeStruct((s, 2 * w), BF16), jax.ShapeDtypeStruct(sa.shape, F32),
                   jax.ShapeDtypeStruct(sb.shape, F32), jax.ShapeDtypeStruct(sh.shape, F32)],
        scratch_shapes=[pltpu.VMEM((s, w), F32), pltpu.VMEM((s, w), F32)],
        compiler_params=_cp(("arbitrary",)), name="even_sample",
    )(z, z, z, z, z, sa, sb, sh, caw, cbw, cbb.reshape(1, w), wai, ba.reshape(1, w), bx.reshape(1, w),
      lam.reshape(1, w))


def _gla_log_decay(gr_ref, wa2_ref, ba_ref):
    pre = jnp.dot(gr_ref[...].astype(BF16), wa2_ref[...], preferred_element_type=F32) + ba_ref[...]
    return _log_sigmoid(pre) * (1.0 / GLA_GATE_NORM)


def _head_norm_gate(o, gn_ref, g):
    o = o * lax.rsqrt(jnp.mean(o * o, axis=-1, keepdims=True) + EPS) * gn_ref[...]
    return o * (g * _sigmoid(g))


def _cumsum_rows(tri16, x):
    hi = x.astype(BF16)
    r1 = x - hi.astype(F32)
    mid = r1.astype(BF16)
    lo = (r1 - mid.astype(F32)).astype(BF16)
    dot = lambda t: jnp.dot(tri16, t, preferred_element_type=F32)
    return dot(hi) + dot(mid) + dot(lo)


def _gla_prompt_kernel(q_ref, k_ref, v_ref, g_ref, gr_ref, wa2_ref, ba_ref, gn_ref, o_ref, s_ref, st):
    c = pl.program_id(1)
    n_h, dv, dk = st.shape
    cs = q_ref.shape[0]

    @pl.when(c == 0)
    def _():
        st[...] = jnp.zeros_like(st)

    ga = _gla_log_decay(gr_ref, wa2_ref, ba_ref)
    row = lax.broadcasted_iota(I32, (cs, cs), 0)
    col = lax.broadcasted_iota(I32, (cs, cs), 1)
    causal = row >= col
    b = _cumsum_rows(causal.astype(BF16), ga)
    b_last = b[cs - 1:cs, :]
    k = k_ref[...]
    qe_all = (q_ref[...] * (dk ** -0.5) * jnp.exp(b)).astype(BF16)
    ke_all = (k * jnp.exp(-b)).astype(BF16)
    kd_all = (k * jnp.exp(b_last - b)).astype(BF16)
    decay = jnp.exp(b_last)
    nt = (((1,), (1,)), ((), ()))
    tn = (((0,), (0,)), ((), ()))
    for h in range(n_h):
        sk, sv = slice(h * dk, (h + 1) * dk), slice(h * dv, (h + 1) * dv)
        qe, ke, kd = qe_all[:, sk], ke_all[:, sk], kd_all[:, sk]
        v16 = v_ref[:, sv].astype(BF16)
        att = lax.dot_general(qe, ke, nt, preferred_element_type=F32)
        att = jnp.where(causal, att, 0.0).astype(BF16)
        s_old = st[h]
        o = lax.dot_general(qe, s_old.astype(BF16), nt, preferred_element_type=F32)
        o = o + jnp.dot(att, v16, preferred_element_type=F32)
        st[h] = s_old * decay[:, sk] + lax.dot_general(v16, kd, tn, preferred_element_type=F32)
        o_ref[:, sv] = _head_norm_gate(o, gn_ref, g_ref[:, sv]).astype(o_ref.dtype)

    @pl.when(c == pl.num_programs(1) - 1)
    def _():
        for h in range(n_h):
            s_ref[0, h] = st[h].T


def gla_prompt(z, zg, n_b, n_t, n_h, dk, dv, wa2p, ba, gnorm):
    n = n_b * n_t
    cs = _tile(n_t, (64, 32, 16, 8))
    nc = n_t // cs
    dkt, dvt = n_h * dk, n_h * dv
    assert (2 * dkt) % dvt == 0
    vb0 = (2 * dkt) // dvt
    row = lambda b, c: b * nc + c
    return pl.pallas_call(
        _gla_prompt_kernel, grid=(n_b, nc),
        in_specs=[pl.BlockSpec((cs, dkt), lambda b, c: (row(b, c), 0)),
                  pl.BlockSpec((cs, dkt), lambda b, c: (row(b, c), 1)),
                  pl.BlockSpec((cs, dvt), lambda b, c: (row(b, c), vb0)),
                  pl.BlockSpec((cs, dvt), lambda b, c: (row(b, c), vb0 + 1)),
                  pl.BlockSpec((cs, LANES), lambda b, c: (row(b, c), 0)),
                  _const_spec((LANES, dkt)), _const_spec((1, dkt)), _const_spec((1, dv))],
        out_specs=[pl.BlockSpec((cs, dvt), lambda b, c: (row(b, c), 0)),
                   pl.BlockSpec((1, n_h, dk, dv), lambda b, c: (b, 0, 0, 0))],
        out_shape=[jax.ShapeDtypeStruct((n, dvt), BF16),
                   jax.ShapeDtypeStruct((n_b, n_h, dk, dv), F32)],
        scratch_shapes=[pltpu.VMEM((n_h, dv, dk), F32)],
        compiler_params=_cp(("parallel", "arbitrary")), name="gla_prompt",
    )(z, z, z, z, zg, wa2p, ba.reshape(1, -1), gnorm.reshape(1, dv))


def _gla_sample_kernel(q_ref, k_ref, v_ref, g_ref, gr_ref, wa2_ref, ba_ref, gn_ref, s0_ref,
                       o_ref, s_ref, ta, tk, tq):
    i = pl.program_id(1)
    nblk, dk, bb = ta.shape

    @pl.when(i == 0)
    def _():
        alpha_t = jnp.exp(_gla_log_decay(gr_ref, wa2_ref, ba_ref)).T
        k_t = k_ref[...].T
        q_t = (q_ref[...] * (dk ** -0.5)).T
        for j in range(nblk):
            ta[j] = alpha_t[:, j * bb:(j + 1) * bb]
            tk[j] = k_t[:, j * bb:(j + 1) * bb]
            tq[j] = q_t[:, j * bb:(j + 1) * bb]

    a3, k3, q3 = ta[i], tk[i], tq[i]
    v = v_ref[...]
    rows = []
    for j in range(bb):
        s_new = s0_ref[j, 0] * a3[:, j:j + 1] + k3[:, j:j + 1] * v[j:j + 1, :]
        s_ref[j, 0] = s_new
        rows.append(jnp.sum(q3[:, j:j + 1] * s_new, axis=0, keepdims=True))
    o = jnp.concatenate(rows, axis=0)
    o_ref[...] = _head_norm_gate(o, gn_ref, g_ref[...]).astype(o_ref.dtype)


def gla_sample(z, zg, row0, s0, wa2p, ba, gnorm):
    s, n_h, dk, dv = s0.shape
    bb = _tile(s, (16, 8))
    nblk = s // bb
    assert row0 % s == 0 and (2 * n_h * dk) % dv == 0
    rs, rb = row0 // s, row0 // bb
    kb0 = n_h
    vb0 = (2 * n_h * dk) // dv
    gb0 = vb0 + n_h
    return pl.pallas_call(
        _gla_sample_kernel, grid=(n_h, nblk),
        in_specs=[pl.BlockSpec((s, dk), lambda h, i: (rs, h)),
                  pl.BlockSpec((s, dk), lambda h, i: (rs, kb0 + h)),
                  pl.BlockSpec((bb, dv), lambda h, i: (rb + i, vb0 + h)),
                  pl.BlockSpec((bb, dv), lambda h, i: (rb + i, gb0 + h)),
                  pl.BlockSpec((s, LANES), lambda h, i: (rs, 0)),
                  pl.BlockSpec((LANES, dk), lambda h, i: (0, h)),
                  pl.BlockSpec((1, dk), lambda h, i: (0, h)),
                  pl.BlockSpec((1, dv), lambda h, i: (0, 0)),
                  pl.BlockSpec((bb, 1, dk, dv), lambda h, i: (i, h, 0, 0))],
        out_specs=[pl.BlockSpec((bb, dv), lambda h, i: (i, h)),
                   pl.BlockSpec((bb, 1, dk, dv), lambda h, i: (i, h, 0, 0))],
        out_shape=[jax.ShapeDtypeStruct((s, n_h * dv), BF16), jax.ShapeDtypeStruct(s0.shape, F32)],
        scratch_shapes=[pltpu.VMEM((nblk, dk, bb), F32)] * 3,
        compiler_params=_cp(("arbitrary", "arbitrary")), name="gla_sample",
    )(z, z, z, z, zg, wa2p, ba.reshape(1, -1), gnorm.reshape(1, dv), s0)


MOE_ROWS = 384
MOE_TILES = (256, 128)
MOE_UP_SLOTS = 4
MOE_DN_SLOTS = 3
MOE_AHEAD = 3


def _moe_kernel(ie_ref, istart_ref, icnt_ref, order_ref, xn_hbm, wg_hbm, wu_hbm, wd_hbm, o_hbm,
                xstage, obuf, xb16, hbuf, ring_up, ring_dn, wg16, wu16, wd16,
                sem_x, sem_o, sem_up, sem_dn, *, layer, n_tok):
    it = pl.program_id(0)
    last = pl.num_programs(0) - 1
    e, start, cnt = ie_ref[it], istart_ref[it], icnt_ref[it]
    nxt = jnp.minimum(it + 1, last)
    has_next = (it < last) & (icnt_ref[nxt] > 0)
    e_next, start_next, cnt_next = ie_ref[nxt], istart_ref[nxt], icnt_ref[nxt]
    prv = jnp.maximum(it - 1, 0)
    start_prev, cnt_prev = istart_ref[prv], jnp.where(it > 0, icnt_ref[prv], 0)
    fc, dc = wg16.shape[1], wd16.shape[1]
    nf, nd = hbuf.shape[1] // fc, obuf.shape[1] // dc
    n_chunks = 2 * nf + nd
    big, small = MOE_TILES

    def chunk_copy(k, ex):
        if k < 2 * nf:
            w_hbm = wg_hbm if k % 2 == 0 else wu_hbm
            slot = k % MOE_UP_SLOTS
            return pltpu.make_async_copy(w_hbm.at[layer, ex, :, pl.ds((k // 2) * fc, fc)], ring_up.at[slot],
                                         sem_up.at[slot])
        c = k - 2 * nf
        slot = c % MOE_DN_SLOTS
        return pltpu.make_async_copy(wd_hbm.at[layer, ex, :, pl.ds(c * dc, dc)], ring_dn.at[slot],
                                     sem_dn.at[slot])

    def prefetch(ks):
        for k in ks:
            if k < n_chunks:
                chunk_copy(k, e).start()
            else:
                @pl.when(has_next)
                def _(k=k):
                    chunk_copy(k - n_chunks, e_next).start()

    def gather_copy(r, base):
        tok = order_ref[base + r] // TOP_K
        return pltpu.make_async_copy(xn_hbm.at[pl.ds(tok, 1), :], xstage.at[pl.ds(r, 1), :], sem_x)

    def scatter_copy(r, base):
        a = order_ref[base + r]
        dst = (a % TOP_K) * n_tok + a // TOP_K
        return pltpu.make_async_copy(obuf.at[pl.ds(r, 1), :], o_hbm.at[pl.ds(dst, 1), :], sem_o)

    def for_rows(n, fn):
        def body(r, c):
            fn(r)
            return c
        lax.fori_loop(0, n, body, 0)

    def for_tiles(fn):
        rem = cnt % big
        n_big = cnt // big + (rem > small).astype(I32)

        def body(j, c):
            fn(pl.multiple_of(j * big, big), big)
            return c
        lax.fori_loop(0, n_big, body, 0)

        @pl.when((rem > 0) & (rem <= small))
        def _():
            fn(pl.multiple_of(n_big * big, big), small)

    @pl.when(it == 0)
    def _():
        xstage[...] = jnp.zeros_like(xstage)

    @pl.when((it == 0) & (cnt > 0))
    def _():
        prefetch(range(MOE_AHEAD))
        for_rows(cnt, lambda r: gather_copy(r, start).start())

    @pl.when(cnt > 0)
    def _():
        for_rows(cnt, lambda r: gather_copy(r, start).wait())

        def cast(r0, t):
            xb16[pl.ds(r0, t), :] = xstage[pl.ds(r0, t), :].astype(BF16)
        for_tiles(cast)

        @pl.when(has_next)
        def _():
            for_rows(cnt_next, lambda r: gather_copy(r, start_next).start())

        for c in range(nf):
            chunk_copy(2 * c, e).wait()
            wg16[...] = ring_up[(2 * c) % MOE_UP_SLOTS].astype(BF16)
            chunk_copy(2 * c + 1, e).wait()
            wu16[...] = ring_up[(2 * c + 1) % MOE_UP_SLOTS].astype(BF16)
            prefetch((2 * c + MOE_AHEAD, 2 * c + MOE_AHEAD + 1))

            def up(r0, t, c=c):
                x = xb16[pl.ds(r0, t), :]
                g = jnp.dot(x, wg16[...], preferred_element_type=F32)
                u = jnp.dot(x, wu16[...], preferred_element_type=F32)
                hbuf[pl.ds(r0, t), c * fc:(c + 1) * fc] = (g * _sigmoid(g) * u).astype(BF16)
            for_tiles(up)

        for_rows(cnt_prev, lambda r: scatter_copy(r, start_prev).wait())

        for c in range(nd):
            k = 2 * nf + c
            chunk_copy(k, e).wait()
            wd16[...] = ring_dn[c % MOE_DN_SLOTS].astype(BF16)
            prefetch((k + MOE_AHEAD,))

            def down(r0, t, c=c):
                obuf[pl.ds(r0, t), c * dc:(c + 1) * dc] = jnp.dot(hbuf[pl.ds(r0, t), :], wd16[...],
                                                                 preferred_element_type=F32)
            for_tiles(down)

        for_rows(cnt, lambda r: scatter_copy(r, start).start())

        @pl.when(jnp.logical_not(has_next))
        def _():
            for_rows(cnt, lambda r: scatter_copy(r, start).wait())


def moe_route_tables(eid, n_exp, rows):
    n = eid.shape[0]
    flat = eid.reshape(-1)
    na = flat.shape[0]
    order = jnp.argsort(flat).astype(I32)
    counts = jnp.bincount(flat, length=n_exp).astype(I32)
    gstart = jnp.cumsum(counts) - counts
    n_it = (counts + rows - 1) // rows
    it_cum = jnp.cumsum(n_it)
    it_base = it_cum - n_it
    total = it_cum[-1]
    n_items = n_exp + na // rows
    ii = jnp.arange(n_items, dtype=I32)
    e_i = jnp.minimum(jnp.searchsorted(it_cum, ii, side="right").astype(I32), n_exp - 1)
    valid = ii < total
    e_i = jnp.where(valid, e_i, e_i[jnp.maximum(total - 1, 0)])
    j = ii - it_base[e_i]
    start = jnp.where(valid, gstart[e_i] + j * rows, 0).astype(I32)
    cnt = jnp.where(valid, jnp.minimum(rows, counts[e_i] - j * rows), 0).astype(I32)
    return e_i, start, cnt, order


def moe_ffn(xn, eid, wg, wu, wd, layer):
    n, d = xn.shape
    _, n_exp, _, f = wg.shape
    rows = MOE_ROWS
    fc = _tile(f, (256, 128))
    dc = _tile(d, (1024, 512, 256, 128))
    e_i, start, cnt, order = moe_route_tables(eid, n_exp, rows)
    n_items = e_i.shape[0]
    kern = functools.partial(_moe_kernel, layer=layer, n_tok=n)
    any_spec = pl.BlockSpec(memory_space=pl.ANY)
    grid_spec = pltpu.PrefetchScalarGridSpec(
        num_scalar_prefetch=4, grid=(n_items,),
        in_specs=[any_spec, any_spec, any_spec, any_spec],
        out_specs=any_spec,
        scratch_shapes=[pltpu.VMEM((rows, d), F32), pltpu.VMEM((rows, d), F32), pltpu.VMEM((rows, d), BF16),
                        pltpu.VMEM((rows, f), BF16),
                        pltpu.VMEM((MOE_UP_SLOTS, d, fc), F32), pltpu.VMEM((MOE_DN_SLOTS, f, dc), F32),
                        pltpu.VMEM((d, fc), BF16), pltpu.VMEM((d, fc), BF16), pltpu.VMEM((f, dc), BF16),
                        pltpu.SemaphoreType.DMA(()), pltpu.SemaphoreType.DMA(()),
                        pltpu.SemaphoreType.DMA((MOE_UP_SLOTS,)), pltpu.SemaphoreType.DMA((MOE_DN_SLOTS,))])
    return pl.pallas_call(
        kern, grid_spec=grid_spec,
        out_shape=jax.ShapeDtypeStruct((TOP_K * n, d), F32),
        compiler_params=_cp(("arbitrary",)), name="moe_ffn",
    )(e_i, start, cnt, order, xn, wg, wu, wd)


def moe_layer(x, g, wrg, brg, wre, bre, wg, wu, wd, layer):
    n_groups = wrg.shape[1]
    n_exp = wre.shape[1]
    d = x.shape[1]
    wr = jnp.zeros((d, LANES), BF16).at[:, :n_groups].set(wrg.astype(BF16))
    wr = wr.at[:, n_groups:n_groups + n_exp].set(wre.astype(BF16))
    br = jnp.zeros((1, LANES), F32).at[0, :n_groups].set(brg).at[0, n_groups:n_groups + n_exp].set(bre)
    xn, eid, wts = router(x, g, wr, br, n_groups, n_exp // n_groups)
    o = moe_ffn(xn, eid[:, :TOP_K], wg, wu, wd, layer)
    return o, wts


def kernel(x_prompt, x_sample, state_conv_a, state_conv_b, state_lru_h, state_gla, norm_mix, norm_ffn,
           norm_final, w_in_even, conv_a_w, conv_b_w, conv_b_b, lru_wa, lru_ba, lru_wx, lru_bx, lru_lambda,
           w_out_even, w_in_odd, gla_wa2, gla_ba, gla_norm, w_out_odd, router_group_w, router_group_b,
           router_expert_w, router_expert_b, moe_w_gate, moe_w_up, moe_w_down):
    pb, pt, d = x_prompt.shape
    sb, st_, _ = x_sample.shape
    assert st_ == 1
    n_p = pb * pt
    n = n_p + sb
    depth = norm_mix.shape[0]
    w = state_lru_h.shape[-1]
    n_h, dk, dv = state_gla.shape[2:]

    x = jnp.concatenate([x_prompt.reshape(n_p, d), x_sample.reshape(sb, d)], axis=0)
    new = {k: [] for k in ("pa", "pb", "ph", "ps", "sa", "sb", "sh", "ss")}
    pending = None

    for l in range(depth):
        e = l // 2
        if pending is None:
            xn = rmsnorm_rows(x, norm_mix[l], BF16)
        else:
            x, xn = combine_norm(x, pending[0], pending[1], norm_mix[l])
        if l % 2 == 0:
            z = matmul(xn, w_in_even[e].astype(BF16))
            wai = jnp.concatenate([lru_wa[e], lru_wx[e]], axis=-1).astype(BF16)
            mix = (conv_a_w[e], conv_b_w[e], conv_b_b[e], wai, lru_ba[e], lru_bx[e], lru_lambda[e])
            y_pr, p_a, p_b, p_h = even_prompt(z, pb, pt, *mix)
            ka, kb = conv_a_w.shape[1], conv_b_w.shape[1]
            y_sm, s_a, s_b, s_h = even_sample(z, n_p, state_conv_a[e].reshape(sb, (ka - 1) * w),
                                              state_conv_b[e].reshape(sb, (kb - 1) * w), state_lru_h[e], *mix)
            new["pa"].append(p_a)
            new["pb"].append(p_b)
            new["ph"].append(p_h.reshape(pb, w))
            new["sa"].append(s_a.reshape(sb, ka - 1, w))
            new["sb"].append(s_b.reshape(sb, kb - 1, w))
            new["sh"].append(s_h)
            w_out = w_out_even[e].astype(BF16)
        else:
            n_main = 2 * n_h * dk + 2 * n_h * dv
            rank = w_in_odd.shape[2] - n_main
            z = matmul(xn, w_in_odd[e][:, :n_main].astype(BF16))
            w_gate = jnp.zeros((d, LANES), BF16).at[:, :rank].set(w_in_odd[e][:, n_main:].astype(BF16))
            zg = matmul(xn, w_gate)
            wa2p = jnp.zeros((LANES, n_h * dk), BF16).at[:rank].set(gla_wa2[e].astype(BF16))
            y_pr, p_s = gla_prompt(z, zg, pb, pt, n_h, dk, dv, wa2p, gla_ba[e], gla_norm[e])
            y_sm, s_s = gla_sample(z, zg, n_p, state_gla[e], wa2p, gla_ba[e], gla_norm[e])
            new["ps"].append(p_s)
            new["ss"].append(s_s)
            w_out = w_out_odd[e].astype(BF16)
        x = matmul_add_rows(y_pr, w_out, x, 0)
        x = matmul_add_rows(y_sm, w_out, x, n_p)
        pending = moe_layer(x, norm_ffn[l], router_group_w[l], router_group_b[l], router_expert_w[l],
                            router_expert_b[l], moe_w_gate, moe_w_up, moe_w_down, l)

    y_p = combine_final(x, pending[0], pending[1], norm_final, 0, n_p).reshape(pb, pt, d)
    y_s = combine_final(x, pending[0], pending[1], norm_final, n_p, sb).reshape(sb, 1, d)
    stack = lambda k: jnp.stack(new[k])
    return (y_p, y_s, stack("pa"), stack("pb"), stack("ph"), stack("ps"),
            stack("sa"), stack("sb"), stack("sh"), stack("ss"))
```

```python
import functools

import jax
import jax.numpy as jnp
from jax import lax
from jax.experimental import pallas as pl
from jax.experimental.pallas import tpu as pltpu

F32 = jnp.float32
BF16 = jnp.bfloat16
I32 = jnp.int32

EPS = 1e-6
LRU_C = 8.0
GLA_GATE_NORM = 16.0
TOP_K = 2
LANES = 128
NEG = -1e30
VMEM_LIMIT = 56 * 1024 * 1024


def _cp(sem, vmem=VMEM_LIMIT):
    return pltpu.CompilerParams(dimension_semantics=sem, vmem_limit_bytes=vmem)


def _tile(n, cands):
    for c in cands:
        if n % c == 0:
            return c
    return n


def _sigmoid(x):
    return 1.0 / (1.0 + jnp.exp(-x))


def _softplus(x):
    return jnp.maximum(x, 0.0) + jnp.log(1.0 + jnp.exp(-jnp.abs(x)))


def _log_sigmoid(x):
    return -_softplus(-x)


def _expm1(x):
    u = jnp.exp(x)
    near = jnp.abs(x) < 0.5
    safe_u = jnp.where(near & (u != 1.0), u, 2.0)
    return jnp.where(near, jnp.where(u == 1.0, x, (u - 1.0) * x / jnp.log(safe_u)), u - 1.0)


def _gelu_tanh(x):
    c = 0.7978845608028654
    return 0.5 * x * (1.0 + jnp.tanh(c * (x + 0.044715 * x * x * x)))


def _rms(x, g):
    return x * lax.rsqrt(jnp.mean(x * x, axis=-1, keepdims=True) + EPS) * g


def _norm_kernel(x_ref, g_ref, xn_ref):
    xn_ref[...] = _rms(x_ref[...], g_ref[...]).astype(xn_ref.dtype)


def _combine(x_ref, o0_ref, o1_ref, w_ref):
    w = w_ref[...]
    return x_ref[...] + (o0_ref[...] * w[:, 0:1] + o1_ref[...] * w[:, 1:2])


def _combine_norm_kernel(x_ref, o0_ref, o1_ref, w_ref, g_ref, xnew_ref, xn_ref):
    x = _combine(x_ref, o0_ref, o1_ref, w_ref)
    xnew_ref[...] = x
    xn_ref[...] = _rms(x, g_ref[...]).astype(xn_ref.dtype)


def _combine_final_kernel(x_ref, o0_ref, o1_ref, w_ref, g_ref, y_ref):
    y_ref[...] = _rms(_combine(x_ref, o0_ref, o1_ref, w_ref), g_ref[...])


def _router_kernel(x_ref, g_ref, wr_ref, br_ref, xn_ref, eid_ref, wts_ref, *, n_groups, per_group):
    xn = _rms(x_ref[...], g_ref[...])
    xn_ref[...] = xn
    logits = jnp.dot(xn.astype(BF16), wr_ref[...], preferred_element_type=F32) + br_ref[...]
    tm = logits.shape[0]
    n_exp = n_groups * per_group
    lane = lax.broadcasted_iota(I32, (tm, LANES), 1)
    is_g = lane < n_groups
    lg = jnp.where(is_g, logits, NEG)
    mg = jnp.max(lg, axis=1, keepdims=True)
    gidx = jnp.min(jnp.where(is_g & (lg == mg), lane, LANES), axis=1, keepdims=True)
    pg = 1.0 / jnp.sum(jnp.where(is_g, jnp.exp(lg - mg), 0.0), axis=1, keepdims=True)
    lo = n_groups + gidx * per_group
    in_grp = (lane >= lo) & (lane < lo + per_group) & (lane < n_groups + n_exp)
    le = jnp.where(in_grp, logits, NEG)
    m1 = jnp.max(le, axis=1, keepdims=True)
    i1 = jnp.min(jnp.where(in_grp & (le == m1), lane, LANES), axis=1, keepdims=True)
    rest = in_grp & (lane != i1)
    le2 = jnp.where(rest, logits, NEG)
    m2 = jnp.max(le2, axis=1, keepdims=True)
    i2 = jnp.min(jnp.where(rest & (le2 == m2), lane, LANES), axis=1, keepdims=True)
    t = jnp.exp(m2 - m1)
    w1 = pg / (1.0 + t)
    w2 = pg * t / (1.0 + t)
    eid_ref[...] = jnp.where(lane == 0, i1 - n_groups, jnp.where(lane == 1, i2 - n_groups, 0))
    wts_ref[...] = jnp.where(lane == 0, w1, jnp.where(lane == 1, w2, 0.0))


def _row_spec(tm, d, off=0):
    return pl.BlockSpec((tm, d), lambda i: (i + off, 0))


def _const_spec(shape):
    return pl.BlockSpec(shape, lambda *_: tuple(0 for _ in shape))


def rmsnorm_rows(x, g, out_dtype):
    n, d = x.shape
    tm = _tile(n, (128, 64, 32, 16, 8))
    return pl.pallas_call(
        _norm_kernel, grid=(n // tm,),
        in_specs=[_row_spec(tm, d), _const_spec((1, d))],
        out_specs=_row_spec(tm, d),
        out_shape=jax.ShapeDtypeStruct((n, d), out_dtype),
        compiler_params=_cp(("parallel",)), name="rmsnorm",
    )(x, g.reshape(1, d))


def combine_norm(x, o, w, g):
    n, d = x.shape
    tm = _tile(n, (128, 64, 32, 16, 8))
    nb = n // tm
    return pl.pallas_call(
        _combine_norm_kernel, grid=(nb,),
        in_specs=[_row_spec(tm, d), _row_spec(tm, d), _row_spec(tm, d, nb), _row_spec(tm, LANES),
                  _const_spec((1, d))],
        out_specs=[_row_spec(tm, d), _row_spec(tm, d)],
        out_shape=[jax.ShapeDtypeStruct((n, d), F32), jax.ShapeDtypeStruct((n, d), BF16)],
        compiler_params=_cp(("parallel",)), name="combine_norm",
    )(x, o, o, w, g.reshape(1, d))


def combine_final(x, o, w, g, row0, rows):
    n, d = x.shape
    tm = next(c for c in (128, 64, 32, 16, 8) if rows % c == 0 and row0 % c == 0 and n % c == 0)
    off = row0 // tm
    return pl.pallas_call(
        _combine_final_kernel, grid=(rows // tm,),
        in_specs=[_row_spec(tm, d, off), _row_spec(tm, d, off), _row_spec(tm, d, off + n // tm),
                  _row_spec(tm, LANES, off), _const_spec((1, d))],
        out_specs=_row_spec(tm, d),
        out_shape=jax.ShapeDtypeStruct((rows, d), F32),
        compiler_params=_cp(("parallel",)), name="combine_final",
    )(x, o, o, w, g.reshape(1, d))


def router(x, g, wr, br, n_groups, per_group):
    n, d = x.shape
    tm = _tile(n, (128, 64, 32, 16, 8))
    kern = functools.partial(_router_kernel, n_groups=n_groups, per_group=per_group)
    return pl.pallas_call(
        kern, grid=(n // tm,),
        in_specs=[_row_spec(tm, d), _const_spec((1, d)), _const_spec((d, LANES)), _const_spec((1, LANES))],
        out_specs=[_row_spec(tm, d), _row_spec(tm, LANES), _row_spec(tm, LANES)],
        out_shape=[jax.ShapeDtypeStruct((n, d), F32), jax.ShapeDtypeStruct((n, LANES), I32),
                   jax.ShapeDtypeStruct((n, LANES), F32)],
        compiler_params=_cp(("parallel",)), name="router",
    )(x, g.reshape(1, d), wr, br)


def _mm_kernel(a_ref, b_ref, o_ref):
    o_ref[...] = jnp.dot(a_ref[...], b_ref[...], preferred_element_type=F32).astype(o_ref.dtype)


def _mm_res_kernel(a_ref, b_ref, r_ref, o_ref):
    o_ref[...] = r_ref[...] + jnp.dot(a_ref[...], b_ref[...], preferred_element_type=F32)


MM_TM = (1024, 832, 640, 512, 256, 128, 64, 32, 16)
MM_TN = (1024, 512, 256, 128)


def matmul(a, b, out_dtype=F32):
    m, k = a.shape
    _, n = b.shape
    tm, tn = _tile(m, MM_TM), _tile(n, MM_TN)
    return pl.pallas_call(
        _mm_kernel, grid=(m // tm, n // tn),
        in_specs=[pl.BlockSpec((tm, k), lambda i, j: (i, 0)), pl.BlockSpec((k, tn), lambda i, j: (0, j))],
        out_specs=pl.BlockSpec((tm, tn), lambda i, j: (i, j)),
        out_shape=jax.ShapeDtypeStruct((m, n), out_dtype),
        compiler_params=_cp(("parallel", "parallel")), name="matmul",
    )(a, b)


def matmul_add_rows(a, b, x, row0):
    m, k = a.shape
    _, n = b.shape
    tm, tn = _tile(m, MM_TM), _tile(n, MM_TN)
    assert row0 % tm == 0
    off = row0 // tm
    xs = pl.BlockSpec((tm, tn), lambda i, j: (i + off, j))
    return pl.pallas_call(
        _mm_res_kernel, grid=(m // tm, n // tn),
        in_specs=[pl.BlockSpec((tm, k), lambda i, j: (i, 0)), pl.BlockSpec((k, tn), lambda i, j: (0, j)), xs],
        out_specs=xs,
        out_shape=jax.ShapeDtypeStruct(x.shape, F32),
        input_output_aliases={2: 0},
        compiler_params=_cp(("parallel", "parallel")), name="matmul_add_rows",
    )(a, b, x)


def _lru_gates(xc, wai_ref, ba_ref, bx_ref, lam_ref, a_out, b_out):
    n_heads, hd, _ = wai_ref.shape
    for h in range(n_heads):
        sl = slice(h * hd, (h + 1) * hd)
        xh = xc[:, sl]
        ri = jnp.dot(xh.astype(BF16), wai_ref[h], preferred_element_type=F32)
        r = _sigmoid(ri[:, :hd] + ba_ref[:, sl])
        i = _sigmoid(ri[:, hd:] + bx_ref[:, sl])
        log_a = (-LRU_C) * r * _softplus(-lam_ref[:, sl])
        a_out[:, sl] = jnp.exp(log_a)
        b_out[:, sl] = jnp.sqrt(-_expm1(2.0 * log_a)) * (i * xh)


def _even_prompt_kernel(gb_ref, gc_ref, xa_ref, xr_ref, gr_ref, caw_ref, cbw_ref, cbb_ref, wai_ref,
                        ba_ref, bx_ref, lam_ref, y_ref, pa_ref, pb_ref, ph_ref,
                        ubuf, rbuf, hcar, abuf, bbuf, hbuf):
    t = pl.program_id(1)
    tt, w = gb_ref.shape
    ka, kb = caw_ref.shape[0], cbw_ref.shape[0]

    @pl.when(t == 0)
    def _():
        ubuf[0:8, :] = jnp.zeros((8, w), F32)
        rbuf[0:8, :] = jnp.zeros((8, w), F32)
        hcar[...] = jnp.zeros_like(hcar)

    u = gc_ref[...] * xa_ref[...]
    ubuf[8:8 + tt, :] = u
    v = caw_ref[ka - 1:ka, :] * u
    for k in range(1, ka):
        v = v + caw_ref[ka - 1 - k:ka - k, :] * ubuf[8 - k:8 - k + tt, :]
    y_ref[:, 0:w] = (gb_ref[...] * v).astype(y_ref.dtype)

    xr = xr_ref[...]
    rbuf[8:8 + tt, :] = xr
    xc = cbw_ref[kb - 1:kb, :] * xr + cbb_ref[...]
    for k in range(1, kb):
        xc = xc + cbw_ref[kb - 1 - k:kb - k, :] * rbuf[8 - k:8 - k + tt, :]
    _lru_gates(xc, wai_ref, ba_ref, bx_ref, lam_ref, abuf, bbuf)

    def step(i, h):
        h = abuf[pl.ds(i, 1), :] * h + bbuf[pl.ds(i, 1), :]
        hbuf[pl.ds(i, 1), :] = h
        return h

    h = lax.fori_loop(0, tt, step, hcar[0:1, :], unroll=8)
    hcar[0:1, :] = h
    y_ref[:, w:2 * w] = (_gelu_tanh(gr_ref[...]) * hbuf[...]).astype(y_ref.dtype)

    pa_ref[0] = ubuf[8 + tt - (ka - 1):8 + tt, :]
    pb_ref[0] = rbuf[8 + tt - (kb - 1):8 + tt, :]
    ph_ref[0] = h
    ubuf[0:8, :] = ubuf[tt:tt + 8, :]
    rbuf[0:8, :] = rbuf[tt:tt + 8, :]


def even_prompt(z, n_b, n_t, caw, cbw, cbb, wai, ba, bx, lam):
    n = n_b * n_t
    w = z.shape[1] // 5
    tt = _tile(n_t, (128, 64, 32, 16, 8))
    nt = n_t // tt
    ka, kb = caw.shape[0], cbw.shape[0]

    def zs(c):
        return pl.BlockSpec((tt, w), lambda b, t: (b * nt + t, c))

    return pl.pallas_call(
        _even_prompt_kernel, grid=(n_b, nt),
        in_specs=[zs(0), zs(1), zs(2), zs(3), zs(4), _const_spec((ka, w)), _const_spec((kb, w)),
                  _const_spec((1, w)), _const_spec(wai.shape), _const_spec((1, w)), _const_spec((1, w)),
                  _const_spec((1, w))],
        out_specs=[pl.BlockSpec((tt, 2 * w), lambda b, t: (b * nt + t, 0)),
                   pl.BlockSpec((1, ka - 1, w), lambda b, t: (b, 0, 0)),
                   pl.BlockSpec((1, kb - 1, w), lambda b, t: (b, 0, 0)),
                   pl.BlockSpec((1, 1, w), lambda b, t: (b, 0, 0))],
        out_shape=[jax.ShapeDtypeStruct((n, 2 * w), BF16),
                   jax.ShapeDtypeStruct((n_b, ka - 1, w), F32),
                   jax.ShapeDtypeStruct((n_b, kb - 1, w), F32),
                   jax.ShapeDtypeStruct((n_b, 1, w), F32)],
        scratch_shapes=[pltpu.VMEM((8 + tt, w), F32), pltpu.VMEM((8 + tt, w), F32), pltpu.VMEM((8, w), F32),
                        pltpu.VMEM((tt, w), F32), pltpu.VMEM((tt, w), F32), pltpu.VMEM((tt, w), F32)],
        compiler_params=_cp(("parallel", "arbitrary")), name="even_prompt",
    )(z, z, z, z, z, caw, cbw, cbb.reshape(1, w), wai, ba.reshape(1, w), bx.reshape(1, w), lam.reshape(1, w))


def _even_sample_kernel(gb_ref, gc_ref, xa_ref, xr_ref, gr_ref, sa_ref, sb_ref, sh_ref, caw_ref, cbw_ref,
                        cbb_ref, wai_ref, ba_ref, bx_ref, lam_ref, y_ref, na_ref, nb_ref, nh_ref,
                        abuf, bbuf):
    w = gb_ref.shape[1]
    ka, kb = caw_ref.shape[0], cbw_ref.shape[0]
    u = gc_ref[...] * xa_ref[...]
    v = caw_ref[ka - 1:ka, :] * u
    for k in range(1, ka):
        v = v + caw_ref[ka - 1 - k:ka - k, :] * sa_ref[:, (ka - 1 - k) * w:(ka - k) * w]
    y_ref[:, 0:w] = (gb_ref[...] * v).astype(y_ref.dtype)
    for k in range(ka - 2):
        na_ref[:, k * w:(k + 1) * w] = sa_ref[:, (k + 1) * w:(k + 2) * w]
    na_ref[:, (ka - 2) * w:(ka - 1) * w] = u

    xr = xr_ref[...]
    xc = cbw_ref[kb - 1:kb, :] * xr + cbb_ref[...]
    for k in range(1, kb):
        xc = xc + cbw_ref[kb - 1 - k:kb - k, :] * sb_ref[:, (kb - 1 - k) * w:(kb - k) * w]
    for k in range(kb - 2):
        nb_ref[:, k * w:(k + 1) * w] = sb_ref[:, (k + 1) * w:(k + 2) * w]
    nb_ref[:, (kb - 2) * w:(kb - 1) * w] = xr

    _lru_gates(xc, wai_ref, ba_ref, bx_ref, lam_ref, abuf, bbuf)
    h = abuf[...] * sh_ref[...] + bbuf[...]
    nh_ref[...] = h
    y_ref[:, w:2 * w] = (_gelu_tanh(gr_ref[...]) * h).astype(y_ref.dtype)


def even_sample(z, row0, sa, sb, sh, caw, cbw, cbb, wai, ba, bx, lam):
    s, w = sh.shape
    ka, kb = caw.shape[0], cbw.shape[0]
    assert row0 % s == 0
    rb = row0 // s

    def zs(c):
        return pl.BlockSpec((s, w), lambda i: (rb, c))

    full = lambda shape: pl.BlockSpec(shape, lambda i: tuple(0 for _ in shape))
    return pl.pallas_call(
        _even_sample_kernel, grid=(1,),
        in_specs=[zs(0), zs(1), zs(2), zs(3), zs(4), full(sa.shape), full(sb.shape), full(sh.shape),
                  full((ka, w)), full((kb, w)), full((1, w)), full(wai.shape), full((1, w)), full((1, w)),
                  full((1, w))],
        out_specs=[full((s, 2 * w)), full(sa.shape), full(sb.shape), full(sh.shape)],
        out_shape=[jax.ShapeDtypeStruct((s, 2 * w), BF16), jax.ShapeDtypeStruct(sa.shape, F32),
                   jax.ShapeDtypeStruct(sb.shape, F32), jax.ShapeDtypeStruct(sh.shape, F32)],
        scratch_shapes=[pltpu.VMEM((s, w), F32), pltpu.VMEM((s, w), F32)],
        compiler_params=_cp(("arbitrary",)), name="even_sample",
    )(z, z, z, z, z, sa, sb, sh, caw, cbw, cbb.reshape(1, w), wai, ba.reshape(1, w), bx.reshape(1, w),
      lam.reshape(1, w))


def _gla_log_decay(gr_ref, wa2_ref, ba_ref):
    pre = jnp.dot(gr_ref[...].astype(BF16), wa2_ref[...], preferred_element_type=F32) + ba_ref[...]
    return _log_sigmoid(pre) * (1.0 / GLA_GATE_NORM)


def _head_norm_gate(o, gn_ref, g):
    o = o * lax.rsqrt(jnp.mean(o * o, axis=-1, keepdims=True) + EPS) * gn_ref[...]
    return o * (g * _sigmoid(g))


def _cumsum_rows(tri16, x):
    hi = x.astype(BF16)
    r1 = x - hi.astype(F32)
    mid = r1.astype(BF16)
    lo = (r1 - mid.astype(F32)).astype(BF16)
    dot = lambda t: jnp.dot(tri16, t, preferred_element_type=F32)
    return dot(hi) + dot(mid) + dot(lo)


GLA_SUB = 16
GLA_SAFE_LOG = 60.0


def _gla_rows(r0, cs, q_ref, k_ref, v_ref, g_ref, gn_ref, o_ref, gabuf, st):
    n_h, dv, dk = st.shape
    rows = pl.ds(r0, cs)
    row = lax.broadcasted_iota(I32, (cs, cs), 0)
    col = lax.broadcasted_iota(I32, (cs, cs), 1)
    causal = row >= col
    b = _cumsum_rows(causal.astype(BF16), gabuf[rows, :])
    b_last = b[cs - 1:cs, :]
    k = k_ref[rows, :]
    qe_all = (q_ref[rows, :] * (dk ** -0.5) * jnp.exp(b)).astype(BF16)
    ke_all = (k * jnp.exp(-b)).astype(BF16)
    kd_all = (k * jnp.exp(b_last - b)).astype(BF16)
    decay = jnp.exp(b_last)
    nt = (((1,), (1,)), ((), ()))
    tn = (((0,), (0,)), ((), ()))
    for h in range(n_h):
        sk, sv = slice(h * dk, (h + 1) * dk), slice(h * dv, (h + 1) * dv)
        qe, ke, kd = qe_all[:, sk], ke_all[:, sk], kd_all[:, sk]
        v16 = v_ref[rows, sv].astype(BF16)
        att = lax.dot_general(qe, ke, nt, preferred_element_type=F32)
        att = jnp.where(causal, att, 0.0).astype(BF16)
        s_old = st[h]
        o = lax.dot_general(qe, s_old.astype(BF16), nt, preferred_element_type=F32)
        o = o + jnp.dot(att, v16, preferred_element_type=F32)
        st[h] = s_old * decay[:, sk] + lax.dot_general(v16, kd, tn, preferred_element_type=F32)
        o_ref[rows, sv] = _head_norm_gate(o, gn_ref, g_ref[rows, sv]).astype(o_ref.dtype)


def _gla_prompt_kernel(q_ref, k_ref, v_ref, g_ref, gr_ref, wa2_ref, ba_ref, gn_ref, o_ref, s_ref, gabuf, st):
    c = pl.program_id(1)
    cs = q_ref.shape[0]

    @pl.when(c == 0)
    def _():
        st[...] = jnp.zeros_like(st)

    ga = _gla_log_decay(gr_ref, wa2_ref, ba_ref)
    gabuf[...] = ga
    one_piece = jnp.min(jnp.sum(ga, axis=0, keepdims=True)) > -GLA_SAFE_LOG
    args = (q_ref, k_ref, v_ref, g_ref, gn_ref, o_ref, gabuf, st)

    @pl.when(one_piece)
    def _():
        _gla_rows(0, cs, *args)

    @pl.when(jnp.logical_not(one_piece))
    def _():
        def sub(j, carry):
            _gla_rows(pl.multiple_of(j * GLA_SUB, GLA_SUB), GLA_SUB, *args)
            return carry
        lax.fori_loop(0, cs // GLA_SUB, sub, 0)

    @pl.when(c == pl.num_programs(1) - 1)
    def _():
        for h in range(st.shape[0]):
            s_ref[0, h] = st[h].T


def gla_prompt(z, zg, n_b, n_t, n_h, dk, dv, wa2p, ba, gnorm):
    n = n_b * n_t
    cs = _tile(n_t, (64, 32, 16))
    nc = n_t // cs
    dkt, dvt = n_h * dk, n_h * dv
    assert (2 * dkt) % dvt == 0 and cs % GLA_SUB == 0
    vb0 = (2 * dkt) // dvt
    row = lambda b, c: b * nc + c
    return pl.pallas_call(
        _gla_prompt_kernel, grid=(n_b, nc),
        in_specs=[pl.BlockSpec((cs, dkt), lambda b, c: (row(b, c), 0)),
                  pl.BlockSpec((cs, dkt), lambda b, c: (row(b, c), 1)),
                  pl.BlockSpec((cs, dvt), lambda b, c: (row(b, c), vb0)),
                  pl.BlockSpec((cs, dvt), lambda b, c: (row(b, c), vb0 + 1)),
                  pl.BlockSpec((cs, LANES), lambda b, c: (row(b, c), 0)),
                  _const_spec((LANES, dkt)), _const_spec((1, dkt)), _const_spec((1, dv))],
        out_specs=[pl.BlockSpec((cs, dvt), lambda b, c: (row(b, c), 0)),
                   pl.BlockSpec((1, n_h, dk, dv), lambda b, c: (b, 0, 0, 0))],
        out_shape=[jax.ShapeDtypeStruct((n, dvt), BF16),
                   jax.ShapeDtypeStruct((n_b, n_h, dk, dv), F32)],
        scratch_shapes=[pltpu.VMEM((cs, dkt), F32), pltpu.VMEM((n_h, dv, dk), F32)],
        compiler_params=_cp(("parallel", "arbitrary")), name="gla_prompt",
    )(z, z, z, z, zg, wa2p, ba.reshape(1, -1), gnorm.reshape(1, dv))


def _gla_sample_kernel(q_ref, k_ref, v_ref, g_ref, gr_ref, wa2_ref, ba_ref, gn_ref, s0_ref,
                       o_ref, s_ref, ta, tk, tq):
    i = pl.program_id(1)
    nblk, dk, bb = ta.shape

    @pl.when(i == 0)
    def _():
        alpha_t = jnp.exp(_gla_log_decay(gr_ref, wa2_ref, ba_ref)).T
        k_t = k_ref[...].T
        q_t = (q_ref[...] * (dk ** -0.5)).T
        for j in range(nblk):
            ta[j] = alpha_t[:, j * bb:(j + 1) * bb]
            tk[j] = k_t[:, j * bb:(j + 1) * bb]
            tq[j] = q_t[:, j * bb:(j + 1) * bb]

    a3, k3, q3 = ta[i], tk[i], tq[i]
    v = v_ref[...]
    rows = []
    for j in range(bb):
        s_new = s0_ref[j, 0] * a3[:, j:j + 1] + k3[:, j:j + 1] * v[j:j + 1, :]
        s_ref[j, 0] = s_new
        rows.append(jnp.sum(q3[:, j:j + 1] * s_new, axis=0, keepdims=True))
    o = jnp.concatenate(rows, axis=0)
    o_ref[...] = _head_norm_gate(o, gn_ref, g_ref[...]).astype(o_ref.dtype)


def gla_sample(z, zg, row0, s0, wa2p, ba, gnorm):
    s, n_h, dk, dv = s0.shape
    bb = _tile(s, (16, 8))
    nblk = s // bb
    assert row0 % s == 0 and (2 * n_h * dk) % dv == 0
    rs, rb = row0 // s, row0 // bb
    kb0 = n_h
    vb0 = (2 * n_h * dk) // dv
    gb0 = vb0 + n_h
    return pl.pallas_call(
        _gla_sample_kernel, grid=(n_h, nblk),
        in_specs=[pl.BlockSpec((s, dk), lambda h, i: (rs, h)),
                  pl.BlockSpec((s, dk), lambda h, i: (rs, kb0 + h)),
                  pl.BlockSpec((bb, dv), lambda h, i: (rb + i, vb0 + h)),
                  pl.BlockSpec((bb, dv), lambda h, i: (rb + i, gb0 + h)),
                  pl.BlockSpec((s, LANES), lambda h, i: (rs, 0)),
                  pl.BlockSpec((LANES, dk), lambda h, i: (0, h)),
                  pl.BlockSpec((1, dk), lambda h, i: (0, h)),
                  pl.BlockSpec((1, dv), lambda h, i: (0, 0)),
                  pl.BlockSpec((bb, 1, dk, dv), lambda h, i: (i, h, 0, 0))],
        out_specs=[pl.BlockSpec((bb, dv), lambda h, i: (i, h)),
                   pl.BlockSpec((bb, 1, dk, dv), lambda h, i: (i, h, 0, 0))],
        out_shape=[jax.ShapeDtypeStruct((s, n_h * dv), BF16), jax.ShapeDtypeStruct(s0.shape, F32)],
        scratch_shapes=[pltpu.VMEM((nblk, dk, bb), F32)] * 3,
        compiler_params=_cp(("arbitrary", "arbitrary")), name="gla_sample",
    )(z, z, z, z, zg, wa2p, ba.reshape(1, -1), gnorm.reshape(1, dv), s0)


MOE_ROWS = 384
MOE_TILES = (256, 128)
MOE_UP_SLOTS = 4
MOE_DN_SLOTS = 3
MOE_AHEAD = 3
MOE_ROW_UNROLL = 4


def _moe_kernel(ie_ref, istart_ref, icnt_ref, tok_ref, dst_ref, xn_hbm, wg_hbm, wu_hbm, wd_hbm, o_hbm,
                xstage, obuf, xb16, hbuf, ring_up, ring_dn, wg16, wu16, wd16,
                sem_x, sem_o, sem_up, sem_dn, *, layer):
    it = pl.program_id(0)
    last = pl.num_programs(0) - 1
    e, start, cnt = ie_ref[it], istart_ref[it], icnt_ref[it]
    nxt = jnp.minimum(it + 1, last)
    has_next = (it < last) & (icnt_ref[nxt] > 0)
    e_next, start_next, cnt_next = ie_ref[nxt], istart_ref[nxt], icnt_ref[nxt]
    prv = jnp.maximum(it - 1, 0)
    start_prev, cnt_prev = istart_ref[prv], jnp.where(it > 0, icnt_ref[prv], 0)
    fc, dc = wg16.shape[1], wd16.shape[1]
    nf, nd = hbuf.shape[1] // fc, obuf.shape[1] // dc
    n_chunks = 2 * nf + nd
    big, small = MOE_TILES

    def chunk_copy(k, ex):
        if k < 2 * nf:
            w_hbm = wg_hbm if k % 2 == 0 else wu_hbm
            slot = k % MOE_UP_SLOTS
            return pltpu.make_async_copy(w_hbm.at[layer, ex, :, pl.ds((k // 2) * fc, fc)], ring_up.at[slot],
                                         sem_up.at[slot])
        c = k - 2 * nf
        slot = c % MOE_DN_SLOTS
        return pltpu.make_async_copy(wd_hbm.at[layer, ex, :, pl.ds(c * dc, dc)], ring_dn.at[slot],
                                     sem_dn.at[slot])

    def prefetch(ks):
        for k in ks:
            if k < n_chunks:
                chunk_copy(k, e).start()
            else:
                @pl.when(has_next)
                def _(k=k):
                    chunk_copy(k - n_chunks, e_next).start()

    def gather_copy(r, base):
        return pltpu.make_async_copy(xn_hbm.at[pl.ds(tok_ref[base + r], 1), :], xstage.at[pl.ds(r, 1), :], sem_x)

    def scatter_copy(r, base):
        return pltpu.make_async_copy(obuf.at[pl.ds(r, 1), :], o_hbm.at[pl.ds(dst_ref[base + r], 1), :], sem_o)

    def for_rows(n, fn):
        n_grp = lax.shift_right_logical(n, MOE_ROW_UNROLL.bit_length() - 1)

        def group(g, c):
            for u in range(MOE_ROW_UNROLL):
                fn(g * MOE_ROW_UNROLL + u)
            return c
        lax.fori_loop(0, n_grp, group, 0)

        def single(r, c):
            fn(r)
            return c
        lax.fori_loop(n_grp * MOE_ROW_UNROLL, n, single, 0)

    def for_tiles(fn):
        rem = cnt % big
        n_big = cnt // big + (rem > small).astype(I32)

        def body(j, c):
            fn(pl.multiple_of(j * big, big), big)
            return c
        lax.fori_loop(0, n_big, body, 0)

        @pl.when((rem > 0) & (rem <= small))
        def _():
            fn(pl.multiple_of(n_big * big, big), small)

    @pl.when(it == 0)
    def _():
        xstage[...] = jnp.zeros_like(xstage)

    @pl.when((it == 0) & (cnt > 0))
    def _():
        prefetch(range(MOE_AHEAD))
        for_rows(cnt, lambda r: gather_copy(r, start).start())

    @pl.when(cnt > 0)
    def _():
        for_rows(cnt, lambda r: gather_copy(r, start).wait())

        def cast(r0, t):
            xb16[pl.ds(r0, t), :] = xstage[pl.ds(r0, t), :].astype(BF16)
        for_tiles(cast)

        @pl.when(has_next)
        def _():
            for_rows(cnt_next, lambda r: gather_copy(r, start_next).start())

        for c in range(nf):
            chunk_copy(2 * c, e).wait()
            wg16[...] = ring_up[(2 * c) % MOE_UP_SLOTS].astype(BF16)
            chunk_copy(2 * c + 1, e).wait()
            wu16[...] = ring_up[(2 * c + 1) % MOE_UP_SLOTS].astype(BF16)
            prefetch((2 * c + MOE_AHEAD, 2 * c + MOE_AHEAD + 1))

            def up(r0, t, c=c):
                x = xb16[pl.ds(r0, t), :]
                g = jnp.dot(x, wg16[...], preferred_element_type=F32)
                u = jnp.dot(x, wu16[...], preferred_element_type=F32)
                hbuf[pl.ds(r0, t), c * fc:(c + 1) * fc] = (g * _sigmoid(g) * u).astype(BF16)
            for_tiles(up)

        for_rows(cnt_prev, lambda r: scatter_copy(r, start_prev).wait())

        for c in range(nd):
            k = 2 * nf + c
            chunk_copy(k, e).wait()
            wd16[...] = ring_dn[c % MOE_DN_SLOTS].astype(BF16)
            prefetch((k + MOE_AHEAD,))

            def down(r0, t, c=c):
                obuf[pl.ds(r0, t), c * dc:(c + 1) * dc] = jnp.dot(hbuf[pl.ds(r0, t), :], wd16[...],
                                                                 preferred_element_type=F32)
            for_tiles(down)

        for_rows(cnt, lambda r: scatter_copy(r, start).start())

        @pl.when(jnp.logical_not(has_next))
        def _():
            for_rows(cnt, lambda r: scatter_copy(r, start).wait())


def moe_route_tables(eid, n_exp, rows):
    n = eid.shape[0]
    flat = eid.reshape(-1)
    na = flat.shape[0]
    order = jnp.argsort(flat).astype(I32)
    counts = jnp.bincount(flat, length=n_exp).astype(I32)
    gstart = jnp.cumsum(counts) - counts
    n_it = (counts + rows - 1) // rows
    it_cum = jnp.cumsum(n_it)
    it_base = it_cum - n_it
    total = it_cum[-1]
    n_items = n_exp + na // rows
    ii = jnp.arange(n_items, dtype=I32)
    e_i = jnp.minimum(jnp.searchsorted(it_cum, ii, side="right").astype(I32), n_exp - 1)
    valid = ii < total
    e_i = jnp.where(valid, e_i, e_i[jnp.maximum(total - 1, 0)])
    j = ii - it_base[e_i]
    start = jnp.where(valid, gstart[e_i] + j * rows, 0).astype(I32)
    cnt = jnp.where(valid, jnp.minimum(rows, counts[e_i] - j * rows), 0).astype(I32)
    tok = order // TOP_K
    dst = (order % TOP_K) * n + tok
    return e_i, start, cnt, tok, dst


def moe_ffn(xn, eid, wg, wu, wd, layer):
    n, d = xn.shape
    _, n_exp, _, f = wg.shape
    rows = MOE_ROWS
    fc = _tile(f, (256, 128))
    dc = _tile(d, (1024, 512, 256, 128))
    e_i, start, cnt, tok, dst = moe_route_tables(eid, n_exp, rows)
    n_items = e_i.shape[0]
    kern = functools.partial(_moe_kernel, layer=layer)
    any_spec = pl.BlockSpec(memory_space=pl.ANY)
    grid_spec = pltpu.PrefetchScalarGridSpec(
        num_scalar_prefetch=5, grid=(n_items,),
        in_specs=[any_spec, any_spec, any_spec, any_spec],
        out_specs=any_spec,
        scratch_shapes=[pltpu.VMEM((rows, d), F32), pltpu.VMEM((rows, d), F32), pltpu.VMEM((rows, d), BF16),
                        pltpu.VMEM((rows, f), BF16),
                        pltpu.VMEM((MOE_UP_SLOTS, d, fc), F32), pltpu.VMEM((MOE_DN_SLOTS, f, dc), F32),
                        pltpu.VMEM((d, fc), BF16), pltpu.VMEM((d, fc), BF16), pltpu.VMEM((f, dc), BF16),
                        pltpu.SemaphoreType.DMA(()), pltpu.SemaphoreType.DMA(()),
                        pltpu.SemaphoreType.DMA((MOE_UP_SLOTS,)), pltpu.SemaphoreType.DMA((MOE_DN_SLOTS,))])
    return pl.pallas_call(
        kern, grid_spec=grid_spec,
        out_shape=jax.ShapeDtypeStruct((TOP_K * n, d), F32),
        compiler_params=_cp(("arbitrary",)), name="moe_ffn",
    )(e_i, start, cnt, tok, dst, xn, wg, wu, wd)


def moe_layer(x, g, wrg, brg, wre, bre, wg, wu, wd, layer):
    n_groups = wrg.shape[1]
    n_exp = wre.shape[1]
    d = x.shape[1]
    wr = jnp.zeros((d, LANES), BF16).at[:, :n_groups].set(wrg.astype(BF16))
    wr = wr.at[:, n_groups:n_groups + n_exp].set(wre.astype(BF16))
    br = jnp.zeros((1, LANES), F32).at[0, :n_groups].set(brg).at[0, n_groups:n_groups + n_exp].set(bre)
    xn, eid, wts = router(x, g, wr, br, n_groups, n_exp // n_groups)
    o = moe_ffn(xn, eid[:, :TOP_K], wg, wu, wd, layer)
    return o, wts


def kernel(x_prompt, x_sample, state_conv_a, state_conv_b, state_lru_h, state_gla, norm_mix, norm_ffn,
           norm_final, w_in_even, conv_a_w, conv_b_w, conv_b_b, lru_wa, lru_ba, lru_wx, lru_bx, lru_lambda,
           w_out_even, w_in_odd, gla_wa2, gla_ba, gla_norm, w_out_odd, router_group_w, router_group_b,
           router_expert_w, router_expert_b, moe_w_gate, moe_w_up, moe_w_down):
    pb, pt, d = x_prompt.shape
    sb, st_, _ = x_sample.shape
    assert st_ == 1
    n_p = pb * pt
    n = n_p + sb
    depth = norm_mix.shape[0]
    w = state_lru_h.shape[-1]
    n_h, dk, dv = state_gla.shape[2:]

    x = jnp.concatenate([x_prompt.reshape(n_p, d), x_sample.reshape(sb, d)], axis=0)
    new = {k: [] for k in ("pa", "pb", "ph", "ps", "sa", "sb", "sh", "ss")}
    pending = None

    for l in range(depth):
        e = l // 2
        if pending is None:
            xn = rmsnorm_rows(x, norm_mix[l], BF16)
        else:
            x, xn = combine_norm(x, pending[0], pending[1], norm_mix[l])
        if l % 2 == 0:
            z = matmul(xn, w_in_even[e].astype(BF16))
            wai = jnp.concatenate([lru_wa[e], lru_wx[e]], axis=-1).astype(BF16)
            mix = (conv_a_w[e], conv_b_w[e], conv_b_b[e], wai, lru_ba[e], lru_bx[e], lru_lambda[e])
            y_pr, p_a, p_b, p_h = even_prompt(z, pb, pt, *mix)
            ka, kb = conv_a_w.shape[1], conv_b_w.shape[1]
            y_sm, s_a, s_b, s_h = even_sample(z, n_p, state_conv_a[e].reshape(sb, (ka - 1) * w),
                                              state_conv_b[e].reshape(sb, (kb - 1) * w), state_lru_h[e], *mix)
            new["pa"].append(p_a)
            new["pb"].append(p_b)
            new["ph"].append(p_h.reshape(pb, w))
            new["sa"].append(s_a.reshape(sb, ka - 1, w))
            new["sb"].append(s_b.reshape(sb, kb - 1, w))
            new["sh"].append(s_h)
            w_out = w_out_even[e].astype(BF16)
        else:
            n_main = 2 * n_h * dk + 2 * n_h * dv
            rank = w_in_odd.shape[2] - n_main
            z = matmul(xn, w_in_odd[e][:, :n_main].astype(BF16))
            w_gate = jnp.zeros((d, LANES), BF16).at[:, :rank].set(w_in_odd[e][:, n_main:].astype(BF16))
            zg = matmul(xn, w_gate)
            wa2p = jnp.zeros((LANES, n_h * dk), BF16).at[:rank].set(gla_wa2[e].astype(BF16))
            y_pr, p_s = gla_prompt(z, zg, pb, pt, n_h, dk, dv, wa2p, gla_ba[e], gla_norm[e])
            y_sm, s_s = gla_sample(z, zg, n_p, state_gla[e], wa2p, gla_ba[e], gla_norm[e])
            new["ps"].append(p_s)
            new["ss"].append(s_s)
            w_out = w_out_odd[e].astype(BF16)
        x = matmul_add_rows(y_pr, w_out, x, 0)
        x = matmul_add_rows(y_sm, w_out, x, n_p)
        pending = moe_layer(x, norm_ffn[l], router_group_w[l], router_group_b[l], router_expert_w[l],
                            router_expert_b[l], moe_w_gate, moe_w_up, moe_w_down, l)

    y_p = combine_final(x, pending[0], pending[1], norm_final, 0, n_p).reshape(pb, pt, d)
    y_s = combine_final(x, pending[0], pending[1], norm_final, n_p, sb).reshape(sb, 1, d)
    stack = lambda k: jnp.stack(new[k])
    return (y_p, y_s, stack("pa"), stack("pb"), stack("ph"), stack("ps"),
            stack("sa"), stack("sb"), stack("sh"), stack("ss"))
```

```python
import functools

import jax
import jax.numpy as jnp
from jax import lax
from jax.experimental import pallas as pl
from jax.experimental.pallas import tpu as pltpu

F32 = jnp.float32
BF16 = jnp.bfloat16
I32 = jnp.int32

EPS = 1e-6
LRU_C = 8.0
GLA_GATE_NORM = 16.0
TOP_K = 2
LANES = 128
NEG = -1e30
VMEM_LIMIT = 56 * 1024 * 1024


def _cp(sem, vmem=VMEM_LIMIT):
    return pltpu.CompilerParams(dimension_semantics=sem, vmem_limit_bytes=vmem)


def _tile(n, cands):
    for c in cands:
        if n % c == 0:
            return c
    return n


def _sigmoid(x):
    return 1.0 / (1.0 + jnp.exp(-x))


def _softplus(x):
    return jnp.maximum(x, 0.0) + jnp.log(1.0 + jnp.exp(-jnp.abs(x)))


def _log_sigmoid(x):
    return -_softplus(-x)


def _expm1(x):
    u = jnp.exp(x)
    near = jnp.abs(x) < 0.5
    safe_u = jnp.where(near & (u != 1.0), u, 2.0)
    return jnp.where(near, jnp.where(u == 1.0, x, (u - 1.0) * x / jnp.log(safe_u)), u - 1.0)


def _gelu_tanh(x):
    c = 0.7978845608028654
    return 0.5 * x * (1.0 + jnp.tanh(c * (x + 0.044715 * x * x * x)))


def _rms(x, g):
    return x * lax.rsqrt(jnp.mean(x * x, axis=-1, keepdims=True) + EPS) * g


def _combine(x_ref, o0_ref, o1_ref, w_ref):
    w = w_ref[...]
    return x_ref[...] + (o0_ref[...] * w[:, 0:1] + o1_ref[...] * w[:, 1:2])


def _combine_norm_kernel(x_ref, o0_ref, o1_ref, w_ref, g_ref, xnew_ref, xn_ref):
    x = _combine(x_ref, o0_ref, o1_ref, w_ref)
    xnew_ref[...] = x
    xn_ref[...] = _rms(x, g_ref[...]).astype(xn_ref.dtype)


def _combine_final_kernel(x_ref, o0_ref, o1_ref, w_ref, g_ref, y_ref):
    y_ref[...] = _rms(_combine(x_ref, o0_ref, o1_ref, w_ref), g_ref[...])


def _router_kernel(x_ref, g_ref, wr_ref, br_ref, xn_ref, eid_ref, wts_ref, *, n_groups, per_group):
    xn = _rms(x_ref[...], g_ref[...])
    xn_ref[...] = xn
    logits = jnp.dot(xn.astype(BF16), wr_ref[...], preferred_element_type=F32) + br_ref[...]
    tm = logits.shape[0]
    n_exp = n_groups * per_group
    lane = lax.broadcasted_iota(I32, (tm, LANES), 1)
    is_g = lane < n_groups
    lg = jnp.where(is_g, logits, NEG)
    mg = jnp.max(lg, axis=1, keepdims=True)
    gidx = jnp.min(jnp.where(is_g & (lg == mg), lane, LANES), axis=1, keepdims=True)
    pg = 1.0 / jnp.sum(jnp.where(is_g, jnp.exp(lg - mg), 0.0), axis=1, keepdims=True)
    lo = n_groups + gidx * per_group
    in_grp = (lane >= lo) & (lane < lo + per_group) & (lane < n_groups + n_exp)
    le = jnp.where(in_grp, logits, NEG)
    m1 = jnp.max(le, axis=1, keepdims=True)
    i1 = jnp.min(jnp.where(in_grp & (le == m1), lane, LANES), axis=1, keepdims=True)
    rest = in_grp & (lane != i1)
    le2 = jnp.where(rest, logits, NEG)
    m2 = jnp.max(le2, axis=1, keepdims=True)
    i2 = jnp.min(jnp.where(rest & (le2 == m2), lane, LANES), axis=1, keepdims=True)
    t = jnp.exp(m2 - m1)
    w1 = pg / (1.0 + t)
    w2 = pg * t / (1.0 + t)
    eid_ref[...] = jnp.where(lane == 0, i1 - n_groups, jnp.where(lane == 1, i2 - n_groups, 0))
    wts_ref[...] = jnp.where(lane == 0, w1, jnp.where(lane == 1, w2, 0.0))


def _row_spec(tm, d, off=0):
    return pl.BlockSpec((tm, d), lambda i: (i + off, 0))


def _const_spec(shape):
    return pl.BlockSpec(shape, lambda *_: tuple(0 for _ in shape))


def _stack_norm_kernel(xp_ref, xs_ref, g_ref, x_ref, xn_ref, *, nbp):
    def emit(src_ref):
        x = src_ref[...]
        x_ref[...] = x
        xn_ref[...] = _rms(x, g_ref[...]).astype(xn_ref.dtype)

    pl.when(pl.program_id(0) < nbp)(lambda: emit(xp_ref))
    pl.when(pl.program_id(0) >= nbp)(lambda: emit(xs_ref))


def stack_norm(xp, xs, g):
    (n_p, d), n_s = xp.shape, xs.shape[0]
    tm = next(c for c in (128, 64, 32, 16, 8) if n_p % c == 0 and n_s % c == 0)
    nbp, nbs = n_p // tm, n_s // tm
    kern = functools.partial(_stack_norm_kernel, nbp=nbp)
    return pl.pallas_call(
        kern, grid=(nbp + nbs,),
        in_specs=[pl.BlockSpec((tm, d), lambda i: (jnp.minimum(i, nbp - 1), 0)),
                  pl.BlockSpec((tm, d), lambda i: (jnp.maximum(i - nbp, 0), 0)),
                  _const_spec((1, d))],
        out_specs=[_row_spec(tm, d), _row_spec(tm, d)],
        out_shape=[jax.ShapeDtypeStruct((n_p + n_s, d), F32), jax.ShapeDtypeStruct((n_p + n_s, d), BF16)],
        compiler_params=_cp(("parallel",)), name="stack_norm",
    )(xp, xs, g.reshape(1, d))


def combine_norm(x, o, w, g):
    n, d = x.shape
    tm = _tile(n, (128, 64, 32, 16, 8))
    nb = n // tm
    return pl.pallas_call(
        _combine_norm_kernel, grid=(nb,),
        in_specs=[_row_spec(tm, d), _row_spec(tm, d), _row_spec(tm, d, nb), _row_spec(tm, LANES),
                  _const_spec((1, d))],
        out_specs=[_row_spec(tm, d), _row_spec(tm, d)],
        out_shape=[jax.ShapeDtypeStruct((n, d), F32), jax.ShapeDtypeStruct((n, d), BF16)],
        compiler_params=_cp(("parallel",)), name="combine_norm",
    )(x, o, o, w, g.reshape(1, d))


def combine_final(x, o, w, g, row0, rows):
    n, d = x.shape
    tm = next(c for c in (128, 64, 32, 16, 8) if rows % c == 0 and row0 % c == 0 and n % c == 0)
    off = row0 // tm
    return pl.pallas_call(
        _combine_final_kernel, grid=(rows // tm,),
        in_specs=[_row_spec(tm, d, off), _row_spec(tm, d, off), _row_spec(tm, d, off + n // tm),
                  _row_spec(tm, LANES, off), _const_spec((1, d))],
        out_specs=_row_spec(tm, d),
        out_shape=jax.ShapeDtypeStruct((rows, d), F32),
        compiler_params=_cp(("parallel",)), name="combine_final",
    )(x, o, o, w, g.reshape(1, d))


def router(x, g, wr, br, n_groups, per_group):
    n, d = x.shape
    tm = _tile(n, (128, 64, 32, 16, 8))
    kern = functools.partial(_router_kernel, n_groups=n_groups, per_group=per_group)
    return pl.pallas_call(
        kern, grid=(n // tm,),
        in_specs=[_row_spec(tm, d), _const_spec((1, d)), _const_spec((d, LANES)), _const_spec((1, LANES))],
        out_specs=[_row_spec(tm, d), _row_spec(tm, LANES), _row_spec(tm, LANES)],
        out_shape=[jax.ShapeDtypeStruct((n, d), F32), jax.ShapeDtypeStruct((n, LANES), I32),
                   jax.ShapeDtypeStruct((n, LANES), F32)],
        compiler_params=_cp(("parallel",)), name="router",
    )(x, g.reshape(1, d), wr, br)


def _mm_kernel(a_ref, b_ref, o_ref):
    o_ref[...] = jnp.dot(a_ref[...], b_ref[...], preferred_element_type=F32).astype(o_ref.dtype)


def _mm_res_kernel(a_ref, b_ref, r_ref, o_ref):
    o_ref[...] = r_ref[...] + jnp.dot(a_ref[...], b_ref[...], preferred_element_type=F32)


MM_TM = (1024, 832, 640, 512, 256, 128, 64, 32, 16)
MM_TN = (1024, 512, 256, 128)


def matmul(a, b, out_dtype=F32, n=None):
    m, k = a.shape
    n = b.shape[1] if n is None else n
    tm, tn = _tile(m, MM_TM), _tile(n, MM_TN)
    return pl.pallas_call(
        _mm_kernel, grid=(m // tm, n // tn),
        in_specs=[pl.BlockSpec((tm, k), lambda i, j: (i, 0)), pl.BlockSpec((k, tn), lambda i, j: (0, j))],
        out_specs=pl.BlockSpec((tm, tn), lambda i, j: (i, j)),
        out_shape=jax.ShapeDtypeStruct((m, n), out_dtype),
        compiler_params=_cp(("parallel", "parallel")), name="matmul",
    )(a, b)


def matmul_add_rows(a, b, x, row0):
    m, k = a.shape
    _, n = b.shape
    tm, tn = _tile(m, MM_TM), _tile(n, MM_TN)
    assert row0 % tm == 0
    off = row0 // tm
    xs = pl.BlockSpec((tm, tn), lambda i, j: (i + off, j))
    return pl.pallas_call(
        _mm_res_kernel, grid=(m // tm, n // tn),
        in_specs=[pl.BlockSpec((tm, k), lambda i, j: (i, 0)), pl.BlockSpec((k, tn), lambda i, j: (0, j)), xs],
        out_specs=xs,
        out_shape=jax.ShapeDtypeStruct(x.shape, F32),
        input_output_aliases={2: 0},
        compiler_params=_cp(("parallel", "parallel")), name="matmul_add_rows",
    )(a, b, x)


def _lru_gates(xc, wai_ref, ba_ref, bx_ref, lam_ref, a_out, b_out):
    n_heads, hd, _ = wai_ref.shape
    for h in range(n_heads):
        sl = slice(h * hd, (h + 1) * hd)
        xh = xc[:, sl]
        ri = jnp.dot(xh.astype(BF16), wai_ref[h], preferred_element_type=F32)
        r = _sigmoid(ri[:, :hd] + ba_ref[:, sl])
        i = _sigmoid(ri[:, hd:] + bx_ref[:, sl])
        log_a = (-LRU_C) * r * _softplus(-lam_ref[:, sl])
        a_out[:, sl] = jnp.exp(log_a)
        b_out[:, sl] = jnp.sqrt(-_expm1(2.0 * log_a)) * (i * xh)


def _even_prompt_kernel(gb_ref, gc_ref, xa_ref, xr_ref, gr_ref, caw_ref, cbw_ref, cbb_ref, wai_ref,
                        ba_ref, bx_ref, lam_ref, y_ref, pa_ref, pb_ref, ph_ref,
                        ubuf, rbuf, hcar, abuf, bbuf, hbuf):
    t = pl.program_id(1)
    tt, w = gb_ref.shape
    ka, kb = caw_ref.shape[0], cbw_ref.shape[0]

    @pl.when(t == 0)
    def _():
        ubuf[0:8, :] = jnp.zeros((8, w), F32)
        rbuf[0:8, :] = jnp.zeros((8, w), F32)
        hcar[...] = jnp.zeros_like(hcar)

    u = gc_ref[...] * xa_ref[...]
    ubuf[8:8 + tt, :] = u
    v = caw_ref[ka - 1:ka, :] * u
    for k in range(1, ka):
        v = v + caw_ref[ka - 1 - k:ka - k, :] * ubuf[8 - k:8 - k + tt, :]
    y_ref[:, 0:w] = (gb_ref[...] * v).astype(y_ref.dtype)

    xr = xr_ref[...]
    rbuf[8:8 + tt, :] = xr
    xc = cbw_ref[kb - 1:kb, :] * xr + cbb_ref[...]
    for k in range(1, kb):
        xc = xc + cbw_ref[kb - 1 - k:kb - k, :] * rbuf[8 - k:8 - k + tt, :]
    _lru_gates(xc, wai_ref, ba_ref, bx_ref, lam_ref, abuf, bbuf)

    def step(i, h):
        h = abuf[pl.ds(i, 1), :] * h + bbuf[pl.ds(i, 1), :]
        hbuf[pl.ds(i, 1), :] = h
        return h

    h = lax.fori_loop(0, tt, step, hcar[0:1, :], unroll=8)
    hcar[0:1, :] = h
    y_ref[:, w:2 * w] = (_gelu_tanh(gr_ref[...]) * hbuf[...]).astype(y_ref.dtype)

    pa_ref[0] = ubuf[8 + tt - (ka - 1):8 + tt, :]
    pb_ref[0] = rbuf[8 + tt - (kb - 1):8 + tt, :]
    ph_ref[0] = h
    ubuf[0:8, :] = ubuf[tt:tt + 8, :]
    rbuf[0:8, :] = rbuf[tt:tt + 8, :]


def even_prompt(z, n_b, n_t, caw, cbw, cbb, wai, ba, bx, lam):
    n = n_b * n_t
    w = z.shape[1] // 5
    tt = _tile(n_t, (128, 64, 32, 16, 8))
    nt = n_t // tt
    ka, kb = caw.shape[0], cbw.shape[0]

    def zs(c):
        return pl.BlockSpec((tt, w), lambda b, t: (b * nt + t, c))

    return pl.pallas_call(
        _even_prompt_kernel, grid=(n_b, nt),
        in_specs=[zs(0), zs(1), zs(2), zs(3), zs(4), _const_spec((ka, w)), _const_spec((kb, w)),
                  _const_spec((1, w)), _const_spec(wai.shape), _const_spec((1, w)), _const_spec((1, w)),
                  _const_spec((1, w))],
        out_specs=[pl.BlockSpec((tt, 2 * w), lambda b, t: (b * nt + t, 0)),
                   pl.BlockSpec((1, ka - 1, w), lambda b, t: (b, 0, 0)),
                   pl.BlockSpec((1, kb - 1, w), lambda b, t: (b, 0, 0)),
                   pl.BlockSpec((1, 1, w), lambda b, t: (b, 0, 0))],
        out_shape=[jax.ShapeDtypeStruct((n, 2 * w), BF16),
                   jax.ShapeDtypeStruct((n_b, ka - 1, w), F32),
                   jax.ShapeDtypeStruct((n_b, kb - 1, w), F32),
                   jax.ShapeDtypeStruct((n_b, 1, w), F32)],
        scratch_shapes=[pltpu.VMEM((8 + tt, w), F32), pltpu.VMEM((8 + tt, w), F32), pltpu.VMEM((8, w), F32),
                        pltpu.VMEM((tt, w), F32), pltpu.VMEM((tt, w), F32), pltpu.VMEM((tt, w), F32)],
        compiler_params=_cp(("parallel", "arbitrary")), name="even_prompt",
    )(z, z, z, z, z, caw, cbw, cbb.reshape(1, w), wai, ba.reshape(1, w), bx.reshape(1, w), lam.reshape(1, w))


def _even_sample_kernel(gb_ref, gc_ref, xa_ref, xr_ref, gr_ref, sa_ref, sb_ref, sh_ref, caw_ref, cbw_ref,
                        cbb_ref, wai_ref, ba_ref, bx_ref, lam_ref, y_ref, na_ref, nb_ref, nh_ref,
                        abuf, bbuf):
    w = gb_ref.shape[1]
    ka, kb = caw_ref.shape[0], cbw_ref.shape[0]
    u = gc_ref[...] * xa_ref[...]
    v = caw_ref[ka - 1:ka, :] * u
    for k in range(1, ka):
        v = v + caw_ref[ka - 1 - k:ka - k, :] * sa_ref[:, (ka - 1 - k) * w:(ka - k) * w]
    y_ref[:, 0:w] = (gb_ref[...] * v).astype(y_ref.dtype)
    for k in range(ka - 2):
        na_ref[:, k * w:(k + 1) * w] = sa_ref[:, (k + 1) * w:(k + 2) * w]
    na_ref[:, (ka - 2) * w:(ka - 1) * w] = u

    xr = xr_ref[...]
    xc = cbw_ref[kb - 1:kb, :] * xr + cbb_ref[...]
    for k in range(1, kb):
        xc = xc + cbw_ref[kb - 1 - k:kb - k, :] * sb_ref[:, (kb - 1 - k) * w:(kb - k) * w]
    for k in range(kb - 2):
        nb_ref[:, k * w:(k + 1) * w] = sb_ref[:, (k + 1) * w:(k + 2) * w]
    nb_ref[:, (kb - 2) * w:(kb - 1) * w] = xr

    _lru_gates(xc, wai_ref, ba_ref, bx_ref, lam_ref, abuf, bbuf)
    h = abuf[...] * sh_ref[...] + bbuf[...]
    nh_ref[...] = h
    y_ref[:, w:2 * w] = (_gelu_tanh(gr_ref[...]) * h).astype(y_ref.dtype)


def even_sample(z, row0, sa, sb, sh, caw, cbw, cbb, wai, ba, bx, lam):
    s, w = sh.shape
    ka, kb = caw.shape[0], cbw.shape[0]
    assert row0 % s == 0
    rb = row0 // s

    def zs(c):
        return pl.BlockSpec((s, w), lambda i: (rb, c))

    full = lambda shape: pl.BlockSpec(shape, lambda i: tuple(0 for _ in shape))
    return pl.pallas_call(
        _even_sample_kernel, grid=(1,),
        in_specs=[zs(0), zs(1), zs(2), zs(3), zs(4), full(sa.shape), full(sb.shape), full(sh.shape),
                  full((ka, w)), full((kb, w)), full((1, w)), full(wai.shape), full((1, w)), full((1, w)),
                  full((1, w))],
        out_specs=[full((s, 2 * w)), full(sa.shape), full(sb.shape), full(sh.shape)],
        out_shape=[jax.ShapeDtypeStruct((s, 2 * w), BF16), jax.ShapeDtypeStruct(sa.shape, F32),
                   jax.ShapeDtypeStruct(sb.shape, F32), jax.ShapeDtypeStruct(sh.shape, F32)],
        scratch_shapes=[pltpu.VMEM((s, w), F32), pltpu.VMEM((s, w), F32)],
        compiler_params=_cp(("arbitrary",)), name="even_sample",
    )(z, z, z, z, z, sa, sb, sh, caw, cbw, cbb.reshape(1, w), wai, ba.reshape(1, w), bx.reshape(1, w),
      lam.reshape(1, w))


def _gla_log_decay(gr_ref, wa2_ref, ba_ref):
    pre = jnp.dot(gr_ref[...].astype(BF16), wa2_ref[...], preferred_element_type=F32) + ba_ref[...]
    return _log_sigmoid(pre) * (1.0 / GLA_GATE_NORM)


def _head_norm_gate(o, gn_ref, g):
    o = o * lax.rsqrt(jnp.mean(o * o, axis=-1, keepdims=True) + EPS) * gn_ref[...]
    return o * (g * _sigmoid(g))


def _cumsum_rows(tri16, x):
    hi = x.astype(BF16)
    r1 = x - hi.astype(F32)
    mid = r1.astype(BF16)
    lo = (r1 - mid.astype(F32)).astype(BF16)
    dot = lambda t: jnp.dot(tri16, t, preferred_element_type=F32)
    return dot(hi) + dot(mid) + dot(lo)


GLA_SUB = 16
GLA_SAFE_LOG = 60.0


def _gla_rows(r0, cs, q_ref, k_ref, v_ref, g_ref, gn_ref, o_ref, gabuf, st):
    n_h, dv, dk = st.shape
    rows = pl.ds(r0, cs)
    row = lax.broadcasted_iota(I32, (cs, cs), 0)
    col = lax.broadcasted_iota(I32, (cs, cs), 1)
    causal = row >= col
    b = _cumsum_rows(causal.astype(BF16), gabuf[rows, :])
    b_last = b[cs - 1:cs, :]
    k = k_ref[rows, :]
    qe_all = (q_ref[rows, :] * (dk ** -0.5) * jnp.exp(b)).astype(BF16)
    ke_all = (k * jnp.exp(-b)).astype(BF16)
    kd_all = (k * jnp.exp(b_last - b)).astype(BF16)
    decay = jnp.exp(b_last)
    nt = (((1,), (1,)), ((), ()))
    tn = (((0,), (0,)), ((), ()))
    for h in range(n_h):
        sk, sv = slice(h * dk, (h + 1) * dk), slice(h * dv, (h + 1) * dv)
        qe, ke, kd = qe_all[:, sk], ke_all[:, sk], kd_all[:, sk]
        v16 = v_ref[rows, sv].astype(BF16)
        att = lax.dot_general(qe, ke, nt, preferred_element_type=F32)
        att = jnp.where(causal, att, 0.0).astype(BF16)
        s_old = st[h]
        o = lax.dot_general(qe, s_old.astype(BF16), nt, preferred_element_type=F32)
        o = o + jnp.dot(att, v16, preferred_element_type=F32)
        st[h] = s_old * decay[:, sk] + lax.dot_general(v16, kd, tn, preferred_element_type=F32)
        o_ref[rows, sv] = _head_norm_gate(o, gn_ref, g_ref[rows, sv]).astype(o_ref.dtype)


def _gla_prompt_kernel(q_ref, k_ref, v_ref, g_ref, gr_ref, wa2_ref, ba_ref, gn_ref, o_ref, s_ref, gabuf, st):
    c = pl.program_id(1)
    cs = q_ref.shape[0]

    @pl.when(c == 0)
    def _():
        st[...] = jnp.zeros_like(st)

    ga = _gla_log_decay(gr_ref, wa2_ref, ba_ref)
    gabuf[...] = ga
    one_piece = jnp.min(jnp.sum(ga, axis=0, keepdims=True)) > -GLA_SAFE_LOG
    args = (q_ref, k_ref, v_ref, g_ref, gn_ref, o_ref, gabuf, st)

    @pl.when(one_piece)
    def _():
        _gla_rows(0, cs, *args)

    @pl.when(jnp.logical_not(one_piece))
    def _():
        def sub(j, carry):
            _gla_rows(pl.multiple_of(j * GLA_SUB, GLA_SUB), GLA_SUB, *args)
            return carry
        lax.fori_loop(0, cs // GLA_SUB, sub, 0)

    @pl.when(c == pl.num_programs(1) - 1)
    def _():
        for h in range(st.shape[0]):
            s_ref[0, h] = st[h].T


def gla_prompt(z, zg, n_b, n_t, n_h, dk, dv, wa2p, ba, gnorm):
    n = n_b * n_t
    cs = _tile(n_t, (64, 32, 16))
    nc = n_t // cs
    dkt, dvt = n_h * dk, n_h * dv
    assert (2 * dkt) % dvt == 0 and cs % GLA_SUB == 0
    vb0 = (2 * dkt) // dvt
    row = lambda b, c: b * nc + c
    return pl.pallas_call(
        _gla_prompt_kernel, grid=(n_b, nc),
        in_specs=[pl.BlockSpec((cs, dkt), lambda b, c: (row(b, c), 0)),
                  pl.BlockSpec((cs, dkt), lambda b, c: (row(b, c), 1)),
                  pl.BlockSpec((cs, dvt), lambda b, c: (row(b, c), vb0)),
                  pl.BlockSpec((cs, dvt), lambda b, c: (row(b, c), vb0 + 1)),
                  pl.BlockSpec((cs, LANES), lambda b, c: (row(b, c), 0)),
                  _const_spec((LANES, dkt)), _const_spec((1, dkt)), _const_spec((1, dv))],
        out_specs=[pl.BlockSpec((cs, dvt), lambda b, c: (row(b, c), 0)),
                   pl.BlockSpec((1, n_h, dk, dv), lambda b, c: (b, 0, 0, 0))],
        out_shape=[jax.ShapeDtypeStruct((n, dvt), BF16),
                   jax.ShapeDtypeStruct((n_b, n_h, dk, dv), F32)],
        scratch_shapes=[pltpu.VMEM((cs, dkt), F32), pltpu.VMEM((n_h, dv, dk), F32)],
        compiler_params=_cp(("parallel", "arbitrary")), name="gla_prompt",
    )(z, z, z, z, zg, wa2p, ba.reshape(1, -1), gnorm.reshape(1, dv))


def _gla_sample_kernel(q_ref, k_ref, v_ref, g_ref, gr_ref, wa2_ref, ba_ref, gn_ref, s0_ref,
                       o_ref, s_ref, ta, tk, tq):
    i = pl.program_id(1)
    nblk, dk, bb = ta.shape

    @pl.when(i == 0)
    def _():
        alpha_t = jnp.exp(_gla_log_decay(gr_ref, wa2_ref, ba_ref)).T
        k_t = k_ref[...].T
        q_t = (q_ref[...] * (dk ** -0.5)).T
        for j in range(nblk):
            ta[j] = alpha_t[:, j * bb:(j + 1) * bb]
            tk[j] = k_t[:, j * bb:(j + 1) * bb]
            tq[j] = q_t[:, j * bb:(j + 1) * bb]

    a3, k3, q3 = ta[i], tk[i], tq[i]
    v = v_ref[...]
    rows = []
    for j in range(bb):
        s_new = s0_ref[j, 0] * a3[:, j:j + 1] + k3[:, j:j + 1] * v[j:j + 1, :]
        s_ref[j, 0] = s_new
        rows.append(jnp.sum(q3[:, j:j + 1] * s_new, axis=0, keepdims=True))
    o = jnp.concatenate(rows, axis=0)
    o_ref[...] = _head_norm_gate(o, gn_ref, g_ref[...]).astype(o_ref.dtype)


def gla_sample(z, zg, row0, s0, wa2p, ba, gnorm):
    s, n_h, dk, dv = s0.shape
    bb = _tile(s, (16, 8))
    nblk = s // bb
    assert row0 % s == 0 and (2 * n_h * dk) % dv == 0
    rs, rb = row0 // s, row0 // bb
    kb0 = n_h
    vb0 = (2 * n_h * dk) // dv
    gb0 = vb0 + n_h
    return pl.pallas_call(
        _gla_sample_kernel, grid=(n_h, nblk),
        in_specs=[pl.BlockSpec((s, dk), lambda h, i: (rs, h)),
                  pl.BlockSpec((s, dk), lambda h, i: (rs, kb0 + h)),
                  pl.BlockSpec((bb, dv), lambda h, i: (rb + i, vb0 + h)),
                  pl.BlockSpec((bb, dv), lambda h, i: (rb + i, gb0 + h)),
                  pl.BlockSpec((s, LANES), lambda h, i: (rs, 0)),
                  pl.BlockSpec((LANES, dk), lambda h, i: (0, h)),
                  pl.BlockSpec((1, dk), lambda h, i: (0, h)),
                  pl.BlockSpec((1, dv), lambda h, i: (0, 0)),
                  pl.BlockSpec((bb, 1, dk, dv), lambda h, i: (i, h, 0, 0))],
        out_specs=[pl.BlockSpec((bb, dv), lambda h, i: (i, h)),
                   pl.BlockSpec((bb, 1, dk, dv), lambda h, i: (i, h, 0, 0))],
        out_shape=[jax.ShapeDtypeStruct((s, n_h * dv), BF16), jax.ShapeDtypeStruct(s0.shape, F32)],
        scratch_shapes=[pltpu.VMEM((nblk, dk, bb), F32)] * 3,
        compiler_params=_cp(("arbitrary", "arbitrary")), name="gla_sample",
    )(z, z, z, z, zg, wa2p, ba.reshape(1, -1), gnorm.reshape(1, dv), s0)


MOE_ROWS = 384
MOE_TILES = (256, 128)
MOE_UP_SLOTS = 4
MOE_DN_SLOTS = 3
MOE_AHEAD = 3
MOE_ROW_UNROLL = 8
MOE_WEIGHT_DMA_PRIORITY = 1


def _moe_kernel(ie_ref, istart_ref, icnt_ref, tok_ref, dst_ref, xn_hbm, wg_hbm, wu_hbm, wd_hbm, o_hbm,
                xstage, obuf, xb16, hbuf, ring_up, ring_dn, wg16, wu16, wd16,
                sem_x, sem_o, sem_up, sem_dn, *, layer):
    it = pl.program_id(0)
    last = pl.num_programs(0) - 1
    e, start, cnt = ie_ref[it], istart_ref[it], icnt_ref[it]
    nxt = jnp.minimum(it + 1, last)
    has_next = (it < last) & (icnt_ref[nxt] > 0)
    e_next, start_next, cnt_next = ie_ref[nxt], istart_ref[nxt], icnt_ref[nxt]
    prv = jnp.maximum(it - 1, 0)
    start_prev, cnt_prev = istart_ref[prv], jnp.where(it > 0, icnt_ref[prv], 0)
    fc, dc = wg16.shape[1], wd16.shape[1]
    nf, nd = hbuf.shape[1] // fc, obuf.shape[1] // dc
    n_chunks = 2 * nf + nd
    big, tails = MOE_TILES[0], MOE_TILES[1:]

    def chunk_copy(k, ex):
        if k < 2 * nf:
            w_hbm = wg_hbm if k % 2 == 0 else wu_hbm
            slot = k % MOE_UP_SLOTS
            return pltpu.make_async_copy(w_hbm.at[layer, ex, :, pl.ds((k // 2) * fc, fc)], ring_up.at[slot],
                                         sem_up.at[slot])
        c = k - 2 * nf
        slot = c % MOE_DN_SLOTS
        return pltpu.make_async_copy(wd_hbm.at[layer, ex, :, pl.ds(c * dc, dc)], ring_dn.at[slot],
                                     sem_dn.at[slot])

    def prefetch(ks):
        for k in ks:
            if k < n_chunks:
                chunk_copy(k, e).start(priority=MOE_WEIGHT_DMA_PRIORITY)
            else:
                @pl.when(has_next)
                def _(k=k):
                    chunk_copy(k - n_chunks, e_next).start(priority=MOE_WEIGHT_DMA_PRIORITY)

    def gather_copy(r, base):
        return pltpu.make_async_copy(xn_hbm.at[pl.ds(tok_ref[base + r], 1), :], xstage.at[pl.ds(r, 1), :], sem_x)

    def scatter_copy(r, base):
        return pltpu.make_async_copy(obuf.at[pl.ds(r, 1), :], o_hbm.at[pl.ds(dst_ref[base + r], 1), :], sem_o)

    def for_rows(n, fn):
        n_grp = lax.shift_right_logical(n, MOE_ROW_UNROLL.bit_length() - 1)

        def group(g, c):
            for u in range(MOE_ROW_UNROLL):
                fn(pl.multiple_of(g * MOE_ROW_UNROLL, MOE_ROW_UNROLL) + u)
            return c
        lax.fori_loop(0, n_grp, group, 0)

        def single(r, c):
            fn(r)
            return c
        lax.fori_loop(n_grp * MOE_ROW_UNROLL, n, single, 0)

    def for_tiles(fn):
        rem = cnt % big
        n_big = cnt // big + (rem > tails[0]).astype(I32)

        def body(j, c):
            fn(pl.multiple_of(j * big, big), big)
            return c
        lax.fori_loop(0, n_big, body, 0)

        for t, t_below in zip(tails, tails[1:] + (0,)):
            @pl.when((rem > t_below) & (rem <= t))
            def _(t=t):
                fn(pl.multiple_of(n_big * big, big), t)

    @pl.when(it == 0)
    def _():
        xstage[...] = jnp.zeros_like(xstage)

    @pl.when((it == 0) & (cnt > 0))
    def _():
        prefetch(range(MOE_AHEAD))
        for_rows(cnt, lambda r: gather_copy(r, start).start())

    @pl.when(cnt > 0)
    def _():
        for_rows(cnt, lambda r: gather_copy(r, start).wait())

        def cast(r0, t):
            xb16[pl.ds(r0, t), :] = xstage[pl.ds(r0, t), :].astype(BF16)
        for_tiles(cast)

        @pl.when(has_next)
        def _():
            for_rows(cnt_next, lambda r: gather_copy(r, start_next).start())

        for c in range(nf):
            chunk_copy(2 * c, e).wait()
            wg16[...] = ring_up[(2 * c) % MOE_UP_SLOTS].astype(BF16)
            chunk_copy(2 * c + 1, e).wait()
            wu16[...] = ring_up[(2 * c + 1) % MOE_UP_SLOTS].astype(BF16)
            prefetch((2 * c + MOE_AHEAD, 2 * c + MOE_AHEAD + 1))

            def up(r0, t, c=c):
                x = xb16[pl.ds(r0, t), :]
                g = jnp.dot(x, wg16[...], preferred_element_type=F32)
                u = jnp.dot(x, wu16[...], preferred_element_type=F32)
                hbuf[pl.ds(r0, t), c * fc:(c + 1) * fc] = (g * _sigmoid(g) * u).astype(BF16)
            for_tiles(up)

        for_rows(cnt_prev, lambda r: scatter_copy(r, start_prev).wait())

        for c in range(nd):
            k = 2 * nf + c
            chunk_copy(k, e).wait()
            wd16[...] = ring_dn[c % MOE_DN_SLOTS].astype(BF16)
            prefetch((k + MOE_AHEAD,))

            def down(r0, t, c=c):
                obuf[pl.ds(r0, t), c * dc:(c + 1) * dc] = jnp.dot(hbuf[pl.ds(r0, t), :], wd16[...],
                                                                 preferred_element_type=F32)
            for_tiles(down)

        for_rows(cnt, lambda r: scatter_copy(r, start).start())

        @pl.when(jnp.logical_not(has_next))
        def _():
            for_rows(cnt, lambda r: scatter_copy(r, start).wait())


def moe_route_tables(eid, n_exp, rows):
    n = eid.shape[0]
    flat = eid.reshape(-1)
    na = flat.shape[0]
    order = jnp.argsort(flat).astype(I32)
    counts = jnp.bincount(flat, length=n_exp).astype(I32)
    gstart = jnp.cumsum(counts) - counts
    n_it = (counts + rows - 1) // rows
    it_cum = jnp.cumsum(n_it)
    it_base = it_cum - n_it
    total = it_cum[-1]
    n_items = n_exp + na // rows
    ii = jnp.arange(n_items, dtype=I32)
    e_i = jnp.minimum(jnp.searchsorted(it_cum, ii, side="right").astype(I32), n_exp - 1)
    valid = ii < total
    e_i = jnp.where(valid, e_i, e_i[jnp.maximum(total - 1, 0)])
    j = ii - it_base[e_i]
    start = jnp.where(valid, gstart[e_i] + j * rows, 0).astype(I32)
    cnt = jnp.where(valid, jnp.minimum(rows, counts[e_i] - j * rows), 0).astype(I32)
    tok = order // TOP_K
    dst = (order % TOP_K) * n + tok
    return e_i, start, cnt, tok, dst


def moe_ffn(xn, eid, wg, wu, wd, layer):
    n, d = xn.shape
    _, n_exp, _, f = wg.shape
    rows = MOE_ROWS
    fc = _tile(f, (256, 128))
    dc = _tile(d, (1024, 512, 256, 128))
    e_i, start, cnt, tok, dst = moe_route_tables(eid, n_exp, rows)
    n_items = e_i.shape[0]
    kern = functools.partial(_moe_kernel, layer=layer)
    any_spec = pl.BlockSpec(memory_space=pl.ANY)
    grid_spec = pltpu.PrefetchScalarGridSpec(
        num_scalar_prefetch=5, grid=(n_items,),
        in_specs=[any_spec, any_spec, any_spec, any_spec],
        out_specs=any_spec,
        scratch_shapes=[pltpu.VMEM((rows, d), F32), pltpu.VMEM((rows, d), F32), pltpu.VMEM((rows, d), BF16),
                        pltpu.VMEM((rows, f), BF16),
                        pltpu.VMEM((MOE_UP_SLOTS, d, fc), F32), pltpu.VMEM((MOE_DN_SLOTS, f, dc), F32),
                        pltpu.VMEM((d, fc), BF16), pltpu.VMEM((d, fc), BF16), pltpu.VMEM((f, dc), BF16),
                        pltpu.SemaphoreType.DMA(()), pltpu.SemaphoreType.DMA(()),
                        pltpu.SemaphoreType.DMA((MOE_UP_SLOTS,)), pltpu.SemaphoreType.DMA((MOE_DN_SLOTS,))])
    return pl.pallas_call(
        kern, grid_spec=grid_spec,
        out_shape=jax.ShapeDtypeStruct((TOP_K * n, d), F32),
        compiler_params=_cp(("arbitrary",)), name="moe_ffn",
    )(e_i, start, cnt, tok, dst, xn, wg, wu, wd)


def moe_layer(x, g, wrg, brg, wre, bre, wg, wu, wd, layer):
    n_groups = wrg.shape[1]
    n_exp = wre.shape[1]
    d = x.shape[1]
    wr = jnp.zeros((d, LANES), BF16).at[:, :n_groups].set(wrg.astype(BF16))
    wr = wr.at[:, n_groups:n_groups + n_exp].set(wre.astype(BF16))
    br = jnp.zeros((1, LANES), F32).at[0, :n_groups].set(brg).at[0, n_groups:n_groups + n_exp].set(bre)
    xn, eid, wts = router(x, g, wr, br, n_groups, n_exp // n_groups)
    o = moe_ffn(xn, eid[:, :TOP_K], wg, wu, wd, layer)
    return o, wts


def kernel(x_prompt, x_sample, state_conv_a, state_conv_b, state_lru_h, state_gla, norm_mix, norm_ffn,
           norm_final, w_in_even, conv_a_w, conv_b_w, conv_b_b, lru_wa, lru_ba, lru_wx, lru_bx, lru_lambda,
           w_out_even, w_in_odd, gla_wa2, gla_ba, gla_norm, w_out_odd, router_group_w, router_group_b,
           router_expert_w, router_expert_b, moe_w_gate, moe_w_up, moe_w_down):
    pb, pt, d = x_prompt.shape
    sb, st_, _ = x_sample.shape
    assert st_ == 1
    n_p = pb * pt
    n = n_p + sb
    depth = norm_mix.shape[0]
    w = state_lru_h.shape[-1]
    n_h, dk, dv = state_gla.shape[2:]

    new = {k: [] for k in ("pa", "pb", "ph", "ps", "sa", "sb", "sh", "ss")}
    pending = None

    for l in range(depth):
        e = l // 2
        if pending is None:
            x, xn = stack_norm(x_prompt.reshape(n_p, d), x_sample.reshape(sb, d), norm_mix[l])
        else:
            x, xn = combine_norm(x, pending[0], pending[1], norm_mix[l])
        if l % 2 == 0:
            z = matmul(xn, w_in_even[e].astype(BF16))
            wai = jnp.concatenate([lru_wa[e], lru_wx[e]], axis=-1).astype(BF16)
            mix = (conv_a_w[e], conv_b_w[e], conv_b_b[e], wai, lru_ba[e], lru_bx[e], lru_lambda[e])
            y_pr, p_a, p_b, p_h = even_prompt(z, pb, pt, *mix)
            ka, kb = conv_a_w.shape[1], conv_b_w.shape[1]
            y_sm, s_a, s_b, s_h = even_sample(z, n_p, state_conv_a[e].reshape(sb, (ka - 1) * w),
                                              state_conv_b[e].reshape(sb, (kb - 1) * w), state_lru_h[e], *mix)
            new["pa"].append(p_a)
            new["pb"].append(p_b)
            new["ph"].append(p_h.reshape(pb, w))
            new["sa"].append(s_a.reshape(sb, ka - 1, w))
            new["sb"].append(s_b.reshape(sb, kb - 1, w))
            new["sh"].append(s_h)
            w_out = w_out_even[e].astype(BF16)
        else:
            n_main = 2 * n_h * dk + 2 * n_h * dv
            rank = w_in_odd.shape[2] - n_main
            w_in = w_in_odd[e].astype(BF16)
            z = matmul(xn, w_in, n=n_main)
            w_gate = jnp.zeros((d, LANES), BF16).at[:, :rank].set(w_in[:, n_main:])
            zg = matmul(xn, w_gate)
            wa2p = jnp.zeros((LANES, n_h * dk), BF16).at[:rank].set(gla_wa2[e].astype(BF16))
            y_pr, p_s = gla_prompt(z, zg, pb, pt, n_h, dk, dv, wa2p, gla_ba[e], gla_norm[e])
            y_sm, s_s = gla_sample(z, zg, n_p, state_gla[e], wa2p, gla_ba[e], gla_norm[e])
            new["ps"].append(p_s)
            new["ss"].append(s_s)
            w_out = w_out_odd[e].astype(BF16)
        x = matmul_add_rows(y_pr, w_out, x, 0)
        x = matmul_add_rows(y_sm, w_out, x, n_p)
        pending = moe_layer(x, norm_ffn[l], router_group_w[l], router_group_b[l], router_expert_w[l],
                            router_expert_b[l], moe_w_gate, moe_w_up, moe_w_down, l)

    y_p = combine_final(x, pending[0], pending[1], norm_final, 0, n_p).reshape(pb, pt, d)
    y_s = combine_final(x, pending[0], pending[1], norm_final, n_p, sb).reshape(sb, 1, d)
    stack = lambda k: jnp.stack(new[k])
    return (y_p, y_s, stack("pa"), stack("pb"), stack("ph"), stack("ps"),
            stack("sa"), stack("sb"), stack("sh"), stack("ss"))
```

```python
import functools

import jax
import jax.numpy as jnp
from jax import lax
from jax.experimental import pallas as pl
from jax.experimental.pallas import tpu as pltpu

F32 = jnp.float32
BF16 = jnp.bfloat16
I32 = jnp.int32

EPS = 1e-6
LRU_C = 8.0
GLA_GATE_NORM = 16.0
TOP_K = 2
LANES = 128
NEG = -1e30
VMEM_LIMIT = 56 * 1024 * 1024


def _cp(sem, vmem=VMEM_LIMIT):
    return pltpu.CompilerParams(dimension_semantics=sem, vmem_limit_bytes=vmem)


def _tile(n, cands):
    for c in cands:
        if n % c == 0:
            return c
    return n


def _sigmoid(x):
    return 0.5 * jnp.tanh(0.5 * x) + 0.5


def _softplus(x):
    return jnp.maximum(x, 0.0) + jnp.log(1.0 + jnp.exp(-jnp.abs(x)))


def _log_sigmoid(x):
    return -_softplus(-x)


def _one_minus_exp2(x):
    t = jnp.tanh(x)
    return (-2.0 * t) / (1.0 - t)


def _gelu_tanh(x):
    c = 0.7978845608028654
    return 0.5 * x * (1.0 + jnp.tanh(c * (x + 0.044715 * x * x * x)))


def _rms(x, g):
    return x * lax.rsqrt(jnp.mean(x * x, axis=-1, keepdims=True) + EPS) * g


def _combine(x_ref, o0_ref, o1_ref, w_ref):
    w = w_ref[...]
    return x_ref[...] + (o0_ref[...] * w[:, 0:1] + o1_ref[...] * w[:, 1:2])


def _combine_norm_kernel(x_ref, o0_ref, o1_ref, w_ref, g_ref, xnew_ref, xn_ref):
    x = _combine(x_ref, o0_ref, o1_ref, w_ref)
    xnew_ref[...] = x
    xn_ref[...] = _rms(x, g_ref[...]).astype(xn_ref.dtype)


def _combine_final_kernel(x_ref, o0_ref, o1_ref, w_ref, g_ref, y_ref):
    y_ref[...] = _rms(_combine(x_ref, o0_ref, o1_ref, w_ref), g_ref[...])


def _router_kernel(x_ref, g_ref, wr_ref, br_ref, xn_ref, eid_ref, wts_ref, *, n_groups, per_group):
    xn = _rms(x_ref[...], g_ref[...])
    xn_ref[...] = xn
    logits = jnp.dot(xn.astype(BF16), wr_ref[...], preferred_element_type=F32) + br_ref[...]
    tm = logits.shape[0]
    n_exp = n_groups * per_group
    lane = lax.broadcasted_iota(I32, (tm, LANES), 1)
    is_g = lane < n_groups
    lg = jnp.where(is_g, logits, NEG)
    mg = jnp.max(lg, axis=1, keepdims=True)
    gidx = jnp.min(jnp.where(is_g & (lg == mg), lane, LANES), axis=1, keepdims=True)
    pg = 1.0 / jnp.sum(jnp.where(is_g, jnp.exp(lg - mg), 0.0), axis=1, keepdims=True)
    lo = n_groups + gidx * per_group
    in_grp = (lane >= lo) & (lane < lo + per_group) & (lane < n_groups + n_exp)
    le = jnp.where(in_grp, logits, NEG)
    m1 = jnp.max(le, axis=1, keepdims=True)
    i1 = jnp.min(jnp.where(in_grp & (le == m1), lane, LANES), axis=1, keepdims=True)
    rest = in_grp & (lane != i1)
    le2 = jnp.where(rest, logits, NEG)
    m2 = jnp.max(le2, axis=1, keepdims=True)
    i2 = jnp.min(jnp.where(rest & (le2 == m2), lane, LANES), axis=1, keepdims=True)
    t = jnp.exp(m2 - m1)
    w1 = pg / (1.0 + t)
    w2 = pg * t / (1.0 + t)
    eid_ref[...] = jnp.where(lane == 0, i1 - n_groups, jnp.where(lane == 1, i2 - n_groups, 0))
    wts_ref[...] = jnp.where(lane == 0, w1, jnp.where(lane == 1, w2, 0.0))


def _row_spec(tm, d, off=0):
    return pl.BlockSpec((tm, d), lambda i: (i + off, 0))


def _const_spec(shape):
    return pl.BlockSpec(shape, lambda *_: tuple(0 for _ in shape))


def _stack_norm_kernel(xp_ref, xs_ref, g_ref, x_ref, xn_ref, *, nbp):
    def emit(src_ref):
        x = src_ref[...]
        x_ref[...] = x
        xn_ref[...] = _rms(x, g_ref[...]).astype(xn_ref.dtype)

    pl.when(pl.program_id(0) < nbp)(lambda: emit(xp_ref))
    pl.when(pl.program_id(0) >= nbp)(lambda: emit(xs_ref))


def stack_norm(xp, xs, g):
    (n_p, d), n_s = xp.shape, xs.shape[0]
    tm = next(c for c in (128, 64, 32, 16, 8) if n_p % c == 0 and n_s % c == 0)
    nbp, nbs = n_p // tm, n_s // tm
    kern = functools.partial(_stack_norm_kernel, nbp=nbp)
    return pl.pallas_call(
        kern, grid=(nbp + nbs,),
        in_specs=[pl.BlockSpec((tm, d), lambda i: (jnp.minimum(i, nbp - 1), 0)),
                  pl.BlockSpec((tm, d), lambda i: (jnp.maximum(i - nbp, 0), 0)),
                  _const_spec((1, d))],
        out_specs=[_row_spec(tm, d), _row_spec(tm, d)],
        out_shape=[jax.ShapeDtypeStruct((n_p + n_s, d), F32), jax.ShapeDtypeStruct((n_p + n_s, d), BF16)],
        compiler_params=_cp(("parallel",)), name="stack_norm",
    )(xp, xs, g.reshape(1, d))


def combine_norm(x, o, w, g):
    n, d = x.shape
    tm = _tile(n, (128, 64, 32, 16, 8))
    nb = n // tm
    return pl.pallas_call(
        _combine_norm_kernel, grid=(nb,),
        in_specs=[_row_spec(tm, d), _row_spec(tm, d), _row_spec(tm, d, nb), _row_spec(tm, LANES),
                  _const_spec((1, d))],
        out_specs=[_row_spec(tm, d), _row_spec(tm, d)],
        out_shape=[jax.ShapeDtypeStruct((n, d), F32), jax.ShapeDtypeStruct((n, d), BF16)],
        compiler_params=_cp(("parallel",)), name="combine_norm",
    )(x, o, o, w, g.reshape(1, d))


def combine_final(x, o, w, g, row0, rows):
    n, d = x.shape
    tm = next(c for c in (128, 64, 32, 16, 8) if rows % c == 0 and row0 % c == 0 and n % c == 0)
    off = row0 // tm
    return pl.pallas_call(
        _combine_final_kernel, grid=(rows // tm,),
        in_specs=[_row_spec(tm, d, off), _row_spec(tm, d, off), _row_spec(tm, d, off + n // tm),
                  _row_spec(tm, LANES, off), _const_spec((1, d))],
        out_specs=_row_spec(tm, d),
        out_shape=jax.ShapeDtypeStruct((rows, d), F32),
        compiler_params=_cp(("parallel",)), name="combine_final",
    )(x, o, o, w, g.reshape(1, d))


def router(x, g, wr, br, n_groups, per_group):
    n, d = x.shape
    tm = _tile(n, (128, 64, 32, 16, 8))
    kern = functools.partial(_router_kernel, n_groups=n_groups, per_group=per_group)
    return pl.pallas_call(
        kern, grid=(n // tm,),
        in_specs=[_row_spec(tm, d), _const_spec((1, d)), _const_spec((d, LANES)), _const_spec((1, LANES))],
        out_specs=[_row_spec(tm, d), _row_spec(tm, LANES), _row_spec(tm, LANES)],
        out_shape=[jax.ShapeDtypeStruct((n, d), F32), jax.ShapeDtypeStruct((n, LANES), I32),
                   jax.ShapeDtypeStruct((n, LANES), F32)],
        compiler_params=_cp(("parallel",)), name="router",
    )(x, g.reshape(1, d), wr, br)


def _mm_kernel(a_ref, b_ref, o_ref):
    o_ref[...] = jnp.dot(a_ref[...], b_ref[...], preferred_element_type=F32).astype(o_ref.dtype)


def _mm_res_kernel(a_ref, b_ref, r_ref, o_ref):
    o_ref[...] = r_ref[...] + jnp.dot(a_ref[...], b_ref[...], preferred_element_type=F32)


MM_TM = (1024, 832, 640, 512, 256, 128, 64, 32, 16)
MM_TN = (1024, 512, 256, 128)


def matmul(a, b, out_dtype=F32, n=None):
    m, k = a.shape
    n = b.shape[1] if n is None else n
    tm, tn = _tile(m, MM_TM), _tile(n, MM_TN)
    return pl.pallas_call(
        _mm_kernel, grid=(m // tm, n // tn),
        in_specs=[pl.BlockSpec((tm, k), lambda i, j: (i, 0)), pl.BlockSpec((k, tn), lambda i, j: (0, j))],
        out_specs=pl.BlockSpec((tm, tn), lambda i, j: (i, j)),
        out_shape=jax.ShapeDtypeStruct((m, n), out_dtype),
        compiler_params=_cp(("parallel", "parallel")), name="matmul",
    )(a, b)


def matmul_add_rows(a, b, x, row0):
    m, k = a.shape
    _, n = b.shape
    tm, tn = _tile(m, MM_TM), _tile(n, MM_TN)
    assert row0 % tm == 0
    off = row0 // tm
    xs = pl.BlockSpec((tm, tn), lambda i, j: (i + off, j))
    return pl.pallas_call(
        _mm_res_kernel, grid=(m // tm, n // tn),
        in_specs=[pl.BlockSpec((tm, k), lambda i, j: (i, 0)), pl.BlockSpec((k, tn), lambda i, j: (0, j)), xs],
        out_specs=xs,
        out_shape=jax.ShapeDtypeStruct(x.shape, F32),
        input_output_aliases={2: 0},
        compiler_params=_cp(("parallel", "parallel")), name="matmul_add_rows",
    )(a, b, x)


def _lru_gates(xc, wai_ref, ba_ref, bx_ref, lam_ref, a_out, b_out):
    n_heads, hd, _ = wai_ref.shape
    for h in range(n_heads):
        sl = slice(h * hd, (h + 1) * hd)
        xh = xc[:, sl]
        ri = jnp.dot(xh.astype(BF16), wai_ref[h], preferred_element_type=F32)
        r = _sigmoid(ri[:, :hd] + ba_ref[:, sl])
        i = _sigmoid(ri[:, hd:] + bx_ref[:, sl])
        log_a = (-LRU_C) * r * _softplus(-lam_ref[:, sl])
        a_out[:, sl] = jnp.exp(log_a)
        b_out[:, sl] = jnp.sqrt(_one_minus_exp2(log_a)) * (i * xh)


def _even_prompt_kernel(gb_ref, gc_ref, xa_ref, xr_ref, gr_ref, caw_ref, cbw_ref, cbb_ref, wai_ref,
                        ba_ref, bx_ref, lam_ref, y_ref, pa_ref, pb_ref, ph_ref,
                        ubuf, rbuf, hcar, abuf, bbuf, hbuf):
    t = pl.program_id(1)
    tt, w = gb_ref.shape
    ka, kb = caw_ref.shape[0], cbw_ref.shape[0]

    @pl.when(t == 0)
    def _():
        ubuf[0:8, :] = jnp.zeros((8, w), F32)
        rbuf[0:8, :] = jnp.zeros((8, w), F32)
        hcar[...] = jnp.zeros_like(hcar)

    u = gc_ref[...] * xa_ref[...]
    ubuf[8:8 + tt, :] = u
    v = caw_ref[ka - 1:ka, :] * u
    for k in range(1, ka):
        v = v + caw_ref[ka - 1 - k:ka - k, :] * ubuf[8 - k:8 - k + tt, :]
    y_ref[:, 0:w] = (gb_ref[...] * v).astype(y_ref.dtype)

    xr = xr_ref[...]
    rbuf[8:8 + tt, :] = xr
    xc = cbw_ref[kb - 1:kb, :] * xr + cbb_ref[...]
    for k in range(1, kb):
        xc = xc + cbw_ref[kb - 1 - k:kb - k, :] * rbuf[8 - k:8 - k + tt, :]
    _lru_gates(xc, wai_ref, ba_ref, bx_ref, lam_ref, abuf, bbuf)

    def step(i, h):
        h = abuf[pl.ds(i, 1), :] * h + bbuf[pl.ds(i, 1), :]
        hbuf[pl.ds(i, 1), :] = h
        return h

    h = lax.fori_loop(0, tt, step, hcar[0:1, :], unroll=8)
    hcar[0:1, :] = h
    y_ref[:, w:2 * w] = (_gelu_tanh(gr_ref[...]) * hbuf[...]).astype(y_ref.dtype)

    pa_ref[0] = ubuf[8 + tt - (ka - 1):8 + tt, :]
    pb_ref[0] = rbuf[8 + tt - (kb - 1):8 + tt, :]
    ph_ref[0] = h
    ubuf[0:8, :] = ubuf[tt:tt + 8, :]
    rbuf[0:8, :] = rbuf[tt:tt + 8, :]


def even_prompt(z, n_b, n_t, caw, cbw, cbb, wai, ba, bx, lam):
    n = n_b * n_t
    w = z.shape[1] // 5
    tt = _tile(n_t, (128, 64, 32, 16, 8))
    nt = n_t // tt
    ka, kb = caw.shape[0], cbw.shape[0]

    def zs(c):
        return pl.BlockSpec((tt, w), lambda b, t: (b * nt + t, c))

    return pl.pallas_call(
        _even_prompt_kernel, grid=(n_b, nt),
        in_specs=[zs(0), zs(1), zs(2), zs(3), zs(4), _const_spec((ka, w)), _const_spec((kb, w)),
                  _const_spec((1, w)), _const_spec(wai.shape), _const_spec((1, w)), _const_spec((1, w)),
                  _const_spec((1, w))],
        out_specs=[pl.BlockSpec((tt, 2 * w), lambda b, t: (b * nt + t, 0)),
                   pl.BlockSpec((1, ka - 1, w), lambda b, t: (b, 0, 0)),
                   pl.BlockSpec((1, kb - 1, w), lambda b, t: (b, 0, 0)),
                   pl.BlockSpec((1, 1, w), lambda b, t: (b, 0, 0))],
        out_shape=[jax.ShapeDtypeStruct((n, 2 * w), BF16),
                   jax.ShapeDtypeStruct((n_b, ka - 1, w), F32),
                   jax.ShapeDtypeStruct((n_b, kb - 1, w), F32),
                   jax.ShapeDtypeStruct((n_b, 1, w), F32)],
        scratch_shapes=[pltpu.VMEM((8 + tt, w), F32), pltpu.VMEM((8 + tt, w), F32), pltpu.VMEM((8, w), F32),
                        pltpu.VMEM((tt, w), F32), pltpu.VMEM((tt, w), F32), pltpu.VMEM((tt, w), F32)],
        compiler_params=_cp(("parallel", "arbitrary")), name="even_prompt",
    )(z, z, z, z, z, caw, cbw, cbb.reshape(1, w), wai, ba.reshape(1, w), bx.reshape(1, w), lam.reshape(1, w))


def _even_sample_kernel(gb_ref, gc_ref, xa_ref, xr_ref, gr_ref, sa_ref, sb_ref, sh_ref, caw_ref, cbw_ref,
                        cbb_ref, wai_ref, ba_ref, bx_ref, lam_ref, y_ref, na_ref, nb_ref, nh_ref,
                        abuf, bbuf):
    w = gb_ref.shape[1]
    ka, kb = caw_ref.shape[0], cbw_ref.shape[0]
    u = gc_ref[...] * xa_ref[...]
    v = caw_ref[ka - 1:ka, :] * u
    for k in range(1, ka):
        v = v + caw_ref[ka - 1 - k:ka - k, :] * sa_ref[:, (ka - 1 - k) * w:(ka - k) * w]
    y_ref[:, 0:w] = (gb_ref[...] * v).astype(y_ref.dtype)
    for k in range(ka - 2):
        na_ref[:, k * w:(k + 1) * w] = sa_ref[:, (k + 1) * w:(k + 2) * w]
    na_ref[:, (ka - 2) * w:(ka - 1) * w] = u

    xr = xr_ref[...]
    xc = cbw_ref[kb - 1:kb, :] * xr + cbb_ref[...]
    for k in range(1, kb):
        xc = xc + cbw_ref[kb - 1 - k:kb - k, :] * sb_ref[:, (kb - 1 - k) * w:(kb - k) * w]
    for k in range(kb - 2):
        nb_ref[:, k * w:(k + 1) * w] = sb_ref[:, (k + 1) * w:(k + 2) * w]
    nb_ref[:, (kb - 2) * w:(kb - 1) * w] = xr

    _lru_gates(xc, wai_ref, ba_ref, bx_ref, lam_ref, abuf, bbuf)
    h = abuf[...] * sh_ref[...] + bbuf[...]
    nh_ref[...] = h
    y_ref[:, w:2 * w] = (_gelu_tanh(gr_ref[...]) * h).astype(y_ref.dtype)


def even_sample(z, row0, sa, sb, sh, caw, cbw, cbb, wai, ba, bx, lam):
    s, w = sh.shape
    ka, kb = caw.shape[0], cbw.shape[0]
    assert row0 % s == 0
    rb = row0 // s

    def zs(c):
        return pl.BlockSpec((s, w), lambda i: (rb, c))

    full = lambda shape: pl.BlockSpec(shape, lambda i: tuple(0 for _ in shape))
    return pl.pallas_call(
        _even_sample_kernel, grid=(1,),
        in_specs=[zs(0), zs(1), zs(2), zs(3), zs(4), full(sa.shape), full(sb.shape), full(sh.shape),
                  full((ka, w)), full((kb, w)), full((1, w)), full(wai.shape), full((1, w)), full((1, w)),
                  full((1, w))],
        out_specs=[full((s, 2 * w)), full(sa.shape), full(sb.shape), full(sh.shape)],
        out_shape=[jax.ShapeDtypeStruct((s, 2 * w), BF16), jax.ShapeDtypeStruct(sa.shape, F32),
                   jax.ShapeDtypeStruct(sb.shape, F32), jax.ShapeDtypeStruct(sh.shape, F32)],
        scratch_shapes=[pltpu.VMEM((s, w), F32), pltpu.VMEM((s, w), F32)],
        compiler_params=_cp(("arbitrary",)), name="even_sample",
    )(z, z, z, z, z, sa, sb, sh, caw, cbw, cbb.reshape(1, w), wai, ba.reshape(1, w), bx.reshape(1, w),
      lam.reshape(1, w))


def _gla_log_decay(gr_ref, wa2_ref, ba_ref):
    pre = jnp.dot(gr_ref[...].astype(BF16), wa2_ref[...], preferred_element_type=F32) + ba_ref[...]
    return _log_sigmoid(pre) * (1.0 / GLA_GATE_NORM)


def _head_norm_gate(o, gn_ref, g):
    o = o * lax.rsqrt(jnp.mean(o * o, axis=-1, keepdims=True) + EPS) * gn_ref[...]
    return o * (g * _sigmoid(g))


def _cumsum_rows(tri16, x):
    hi = x.astype(BF16)
    r1 = x - hi.astype(F32)
    mid = r1.astype(BF16)
    lo = (r1 - mid.astype(F32)).astype(BF16)
    dot = lambda t: jnp.dot(tri16, t, preferred_element_type=F32)
    return dot(hi) + dot(mid) + dot(lo)


GLA_SUB = 16
GLA_SAFE_LOG = 60.0


def _gla_rows(r0, cs, q_ref, k_ref, v_ref, g_ref, gn_ref, o_ref, gabuf, st):
    n_h, dv, dk = st.shape
    rows = pl.ds(r0, cs)
    row = lax.broadcasted_iota(I32, (cs, cs), 0)
    col = lax.broadcasted_iota(I32, (cs, cs), 1)
    causal = row >= col
    b = _cumsum_rows(causal.astype(BF16), gabuf[rows, :])
    b_last = b[cs - 1:cs, :]
    k = k_ref[rows, :]
    qe_all = (q_ref[rows, :] * (dk ** -0.5) * jnp.exp(b)).astype(BF16)
    ke_all = (k * jnp.exp(-b)).astype(BF16)
    kd_all = (k * jnp.exp(b_last - b)).astype(BF16)
    decay = jnp.exp(b_last)
    nt = (((1,), (1,)), ((), ()))
    tn = (((0,), (0,)), ((), ()))
    for h in range(n_h):
        sk, sv = slice(h * dk, (h + 1) * dk), slice(h * dv, (h + 1) * dv)
        qe, ke, kd = qe_all[:, sk], ke_all[:, sk], kd_all[:, sk]
        v16 = v_ref[rows, sv].astype(BF16)
        att = lax.dot_general(qe, ke, nt, preferred_element_type=F32)
        att = jnp.where(causal, att, 0.0).astype(BF16)
        s_old = st[h]
        o = lax.dot_general(qe, s_old.astype(BF16), nt, preferred_element_type=F32)
        o = o + jnp.dot(att, v16, preferred_element_type=F32)
        st[h] = s_old * decay[:, sk] + lax.dot_general(v16, kd, tn, preferred_element_type=F32)
        o_ref[rows, sv] = _head_norm_gate(o, gn_ref, g_ref[rows, sv]).astype(o_ref.dtype)


def _gla_prompt_kernel(q_ref, k_ref, v_ref, g_ref, gr_ref, wa2_ref, ba_ref, gn_ref, o_ref, s_ref, gabuf, st):
    c = pl.program_id(1)
    cs = q_ref.shape[0]

    @pl.when(c == 0)
    def _():
        st[...] = jnp.zeros_like(st)

    ga = _gla_log_decay(gr_ref, wa2_ref, ba_ref)
    gabuf[...] = ga
    one_piece = jnp.min(jnp.sum(ga, axis=0, keepdims=True)) > -GLA_SAFE_LOG
    args = (q_ref, k_ref, v_ref, g_ref, gn_ref, o_ref, gabuf, st)

    @pl.when(one_piece)
    def _():
        _gla_rows(0, cs, *args)

    @pl.when(jnp.logical_not(one_piece))
    def _():
        def sub(j, carry):
            _gla_rows(pl.multiple_of(j * GLA_SUB, GLA_SUB), GLA_SUB, *args)
            return carry
        lax.fori_loop(0, cs // GLA_SUB, sub, 0)

    @pl.when(c == pl.num_programs(1) - 1)
    def _():
        for h in range(st.shape[0]):
            s_ref[0, h] = st[h].T


def gla_prompt(z, zg, n_b, n_t, n_h, dk, dv, wa2p, ba, gnorm):
    n = n_b * n_t
    cs = _tile(n_t, (128, 64, 32, 16))
    nc = n_t // cs
    dkt, dvt = n_h * dk, n_h * dv
    assert (2 * dkt) % dvt == 0 and cs % GLA_SUB == 0
    vb0 = (2 * dkt) // dvt
    row = lambda b, c: b * nc + c
    return pl.pallas_call(
        _gla_prompt_kernel, grid=(n_b, nc),
        in_specs=[pl.BlockSpec((cs, dkt), lambda b, c: (row(b, c), 0)),
                  pl.BlockSpec((cs, dkt), lambda b, c: (row(b, c), 1)),
                  pl.BlockSpec((cs, dvt), lambda b, c: (row(b, c), vb0)),
                  pl.BlockSpec((cs, dvt), lambda b, c: (row(b, c), vb0 + 1)),
                  pl.BlockSpec((cs, LANES), lambda b, c: (row(b, c), 0)),
                  _const_spec((LANES, dkt)), _const_spec((1, dkt)), _const_spec((1, dv))],
        out_specs=[pl.BlockSpec((cs, dvt), lambda b, c: (row(b, c), 0)),
                   pl.BlockSpec((1, n_h, dk, dv), lambda b, c: (b, 0, 0, 0))],
        out_shape=[jax.ShapeDtypeStruct((n, dvt), BF16),
                   jax.ShapeDtypeStruct((n_b, n_h, dk, dv), F32)],
        scratch_shapes=[pltpu.VMEM((cs, dkt), F32), pltpu.VMEM((n_h, dv, dk), F32)],
        compiler_params=_cp(("parallel", "arbitrary")), name="gla_prompt",
    )(z, z, z, z, zg, wa2p, ba.reshape(1, -1), gnorm.reshape(1, dv))


def _gla_sample_kernel(q_ref, k_ref, v_ref, g_ref, gr_ref, wa2_ref, ba_ref, gn_ref, s0_ref,
                       o_ref, s_ref, ta, tk, tq):
    i = pl.program_id(1)
    nblk, dk, bb = ta.shape

    @pl.when(i == 0)
    def _():
        alpha_t = jnp.exp(_gla_log_decay(gr_ref, wa2_ref, ba_ref)).T
        k_t = k_ref[...].T
        q_t = (q_ref[...] * (dk ** -0.5)).T
        for j in range(nblk):
            ta[j] = alpha_t[:, j * bb:(j + 1) * bb]
            tk[j] = k_t[:, j * bb:(j + 1) * bb]
            tq[j] = q_t[:, j * bb:(j + 1) * bb]

    a3, k3, q3 = ta[i], tk[i], tq[i]
    v = v_ref[...]
    rows = []
    for j in range(bb):
        s_new = s0_ref[j, 0] * a3[:, j:j + 1] + k3[:, j:j + 1] * v[j:j + 1, :]
        s_ref[j, 0] = s_new
        rows.append(jnp.sum(q3[:, j:j + 1] * s_new, axis=0, keepdims=True))
    o = jnp.concatenate(rows, axis=0)
    o_ref[...] = _head_norm_gate(o, gn_ref, g_ref[...]).astype(o_ref.dtype)


def gla_sample(z, zg, row0, s0, wa2p, ba, gnorm):
    s, n_h, dk, dv = s0.shape
    bb = _tile(s, (16, 8))
    nblk = s // bb
    assert row0 % s == 0 and (2 * n_h * dk) % dv == 0
    rs, rb = row0 // s, row0 // bb
    kb0 = n_h
    vb0 = (2 * n_h * dk) // dv
    gb0 = vb0 + n_h
    return pl.pallas_call(
        _gla_sample_kernel, grid=(n_h, nblk),
        in_specs=[pl.BlockSpec((s, dk), lambda h, i: (rs, h)),
                  pl.BlockSpec((s, dk), lambda h, i: (rs, kb0 + h)),
                  pl.BlockSpec((bb, dv), lambda h, i: (rb + i, vb0 + h)),
                  pl.BlockSpec((bb, dv), lambda h, i: (rb + i, gb0 + h)),
                  pl.BlockSpec((s, LANES), lambda h, i: (rs, 0)),
                  pl.BlockSpec((LANES, dk), lambda h, i: (0, h)),
                  pl.BlockSpec((1, dk), lambda h, i: (0, h)),
                  pl.BlockSpec((1, dv), lambda h, i: (0, 0)),
                  pl.BlockSpec((bb, 1, dk, dv), lambda h, i: (i, h, 0, 0))],
        out_specs=[pl.BlockSpec((bb, dv), lambda h, i: (i, h)),
                   pl.BlockSpec((bb, 1, dk, dv), lambda h, i: (i, h, 0, 0))],
        out_shape=[jax.ShapeDtypeStruct((s, n_h * dv), BF16), jax.ShapeDtypeStruct(s0.shape, F32)],
        scratch_shapes=[pltpu.VMEM((nblk, dk, bb), F32)] * 3,
        compiler_params=_cp(("arbitrary", "arbitrary")), name="gla_sample",
    )(z, z, z, z, zg, wa2p, ba.reshape(1, -1), gnorm.reshape(1, dv), s0)


MOE_SHORT_TILE = 128
MOE_UP_SLOTS = 4
MOE_DN_SLOTS = 3
MOE_AHEAD = 3
MOE_ROW_UNROLL = 8
MOE_WEIGHT_DMA_PRIORITY = 1


def _moe_kernel(ie_ref, istart_ref, icnt_ref, tok_ref, dst_ref, xn_hbm, wg_hbm, wu_hbm, wd_hbm, o_hbm,
                xstage, obuf, xb16, hbuf, ring_up, ring_dn, wg16, wu16, wd16,
                sem_x, sem_o, sem_up, sem_dn, *, layer, tiles):
    it = pl.program_id(0)
    last = pl.num_programs(0) - 1
    e, start, cnt = ie_ref[it], istart_ref[it], icnt_ref[it]
    nxt = jnp.minimum(it + 1, last)
    has_next = (it < last) & (icnt_ref[nxt] > 0)
    e_next, start_next, cnt_next = ie_ref[nxt], istart_ref[nxt], icnt_ref[nxt]
    prv = jnp.maximum(it - 1, 0)
    start_prev, cnt_prev = istart_ref[prv], jnp.where(it > 0, icnt_ref[prv], 0)
    fc, dc = wg16.shape[1], wd16.shape[1]
    nf, nd = hbuf.shape[1] // fc, obuf.shape[1] // dc
    n_chunks = 2 * nf + nd
    big, tails = tiles[0], tiles[1:]

    def chunk_copy(k, ex):
        if k < 2 * nf:
            w_hbm = wg_hbm if k % 2 == 0 else wu_hbm
            slot = k % MOE_UP_SLOTS
            return pltpu.make_async_copy(w_hbm.at[layer, ex, :, pl.ds((k // 2) * fc, fc)], ring_up.at[slot],
                                         sem_up.at[slot])
        c = k - 2 * nf
        slot = c % MOE_DN_SLOTS
        return pltpu.make_async_copy(wd_hbm.at[layer, ex, :, pl.ds(c * dc, dc)], ring_dn.at[slot],
                                     sem_dn.at[slot])

    def prefetch(ks):
        for k in ks:
            if k < n_chunks:
                chunk_copy(k, e).start(priority=MOE_WEIGHT_DMA_PRIORITY)
            else:
                @pl.when(has_next)
                def _(k=k):
                    chunk_copy(k - n_chunks, e_next).start(priority=MOE_WEIGHT_DMA_PRIORITY)

    def gather_copy(r, base):
        return pltpu.make_async_copy(xn_hbm.at[pl.ds(tok_ref[base + r], 1), :], xstage.at[pl.ds(r, 1), :], sem_x)

    def scatter_copy(r, base):
        return pltpu.make_async_copy(obuf.at[pl.ds(r, 1), :], o_hbm.at[pl.ds(dst_ref[base + r], 1), :], sem_o)

    def for_rows(n, fn):
        n_grp = lax.shift_right_logical(n, MOE_ROW_UNROLL.bit_length() - 1)

        def group(g, c):
            for u in range(MOE_ROW_UNROLL):
                fn(pl.multiple_of(g * MOE_ROW_UNROLL, MOE_ROW_UNROLL) + u)
            return c
        lax.fori_loop(0, n_grp, group, 0)

        def single(r, c):
            fn(r)
            return c
        lax.fori_loop(n_grp * MOE_ROW_UNROLL, n, single, 0)

    def for_tiles(fn):
        rem = cnt % big
        n_big = cnt // big + (rem > (tails[0] if tails else 0)).astype(I32)

        def body(j, c):
            fn(pl.multiple_of(j * big, big), big)
            return c
        lax.fori_loop(0, n_big, body, 0)

        for t, t_below in zip(tails, tails[1:] + (0,)):
            @pl.when((rem > t_below) & (rem <= t))
            def _(t=t):
                fn(pl.multiple_of(n_big * big, big), t)

    @pl.when(it == 0)
    def _():
        xstage[...] = jnp.zeros_like(xstage)

    @pl.when((it == 0) & (cnt > 0))
    def _():
        prefetch(range(MOE_AHEAD))
        for_rows(cnt, lambda r: gather_copy(r, start).start())

    @pl.when(cnt > 0)
    def _():
        for_rows(cnt, lambda r: gather_copy(r, start).wait())

        def cast(r0, t):
            xb16[pl.ds(r0, t), :] = xstage[pl.ds(r0, t), :].astype(BF16)
        for_tiles(cast)

        @pl.when(has_next)
        def _():
            for_rows(cnt_next, lambda r: gather_copy(r, start_next).start())

        for c in range(nf):
            chunk_copy(2 * c, e).wait()
            wg16[...] = ring_up[(2 * c) % MOE_UP_SLOTS].astype(BF16)
            chunk_copy(2 * c + 1, e).wait()
            wu16[...] = ring_up[(2 * c + 1) % MOE_UP_SLOTS].astype(BF16)
            prefetch((2 * c + MOE_AHEAD, 2 * c + MOE_AHEAD + 1))

            def up(r0, t, c=c):
                x = xb16[pl.ds(r0, t), :]
                g = jnp.dot(x, wg16[...], preferred_element_type=F32)
                u = jnp.dot(x, wu16[...], preferred_element_type=F32)
                hbuf[pl.ds(r0, t), c * fc:(c + 1) * fc] = (g * _sigmoid(g) * u).astype(BF16)
            for_tiles(up)

        for_rows(cnt_prev, lambda r: scatter_copy(r, start_prev).wait())

        for c in range(nd):
            k = 2 * nf + c
            chunk_copy(k, e).wait()
            wd16[...] = ring_dn[c % MOE_DN_SLOTS].astype(BF16)
            prefetch((k + MOE_AHEAD,))

            def down(r0, t, c=c):
                obuf[pl.ds(r0, t), c * dc:(c + 1) * dc] = jnp.dot(hbuf[pl.ds(r0, t), :], wd16[...],
                                                                 preferred_element_type=F32)
            for_tiles(down)

        for_rows(cnt, lambda r: scatter_copy(r, start).start())

        @pl.when(jnp.logical_not(has_next))
        def _():
            for_rows(cnt, lambda r: scatter_copy(r, start).wait())


def moe_route_tables(eid, n_exp, rows):
    n = eid.shape[0]
    flat = eid.reshape(-1)
    na = flat.shape[0]
    order = jnp.argsort(flat).astype(I32)
    counts = jnp.bincount(flat, length=n_exp).astype(I32)
    gstart = jnp.cumsum(counts) - counts
    n_it = (counts + rows - 1) // rows
    it_cum = jnp.cumsum(n_it)
    it_base = it_cum - n_it
    total = it_cum[-1]
    n_items = n_exp + na // rows
    ii = jnp.arange(n_items, dtype=I32)
    e_i = jnp.minimum(jnp.searchsorted(it_cum, ii, side="right").astype(I32), n_exp - 1)
    valid = ii < total
    e_i = jnp.where(valid, e_i, e_i[jnp.maximum(total - 1, 0)])
    j = ii - it_base[e_i]
    start = jnp.where(valid, gstart[e_i] + j * rows, 0).astype(I32)
    cnt = jnp.where(valid, jnp.minimum(rows, counts[e_i] - j * rows), 0).astype(I32)
    tok = order // TOP_K
    dst = (order % TOP_K) * n + tok
    return e_i, start, cnt, tok, dst


def _moe_tiles(n_assign, n_exp):
    mean = n_assign / n_exp
    big = max(MOE_SHORT_TILE, -(-int(mean + 2.0 * mean ** 0.5) // 32) * 32)
    return (big, MOE_SHORT_TILE) if big > MOE_SHORT_TILE else (big,)


def moe_ffn(xn, eid, wg, wu, wd, layer):
    n, d = xn.shape
    _, n_exp, _, f = wg.shape
    tiles = _moe_tiles(TOP_K * n, n_exp)
    rows = tiles[0]
    fc = _tile(f, (256, 128))
    dc = _tile(d, (1024, 512, 256, 128))
    e_i, start, cnt, tok, dst = moe_route_tables(eid, n_exp, rows)
    n_items = e_i.shape[0]
    kern = functools.partial(_moe_kernel, layer=layer, tiles=tiles)
    any_spec = pl.BlockSpec(memory_space=pl.ANY)
    grid_spec = pltpu.PrefetchScalarGridSpec(
        num_scalar_prefetch=5, grid=(n_items,),
        in_specs=[any_spec, any_spec, any_spec, any_spec],
        out_specs=any_spec,
        scratch_shapes=[pltpu.VMEM((rows, d), F32), pltpu.VMEM((rows, d), F32), pltpu.VMEM((rows, d), BF16),
                        pltpu.VMEM((rows, f), BF16),
                        pltpu.VMEM((MOE_UP_SLOTS, d, fc), F32), pltpu.VMEM((MOE_DN_SLOTS, f, dc), F32),
                        pltpu.VMEM((d, fc), BF16), pltpu.VMEM((d, fc), BF16), pltpu.VMEM((f, dc), BF16),
                        pltpu.SemaphoreType.DMA(()), pltpu.SemaphoreType.DMA(()),
                        pltpu.SemaphoreType.DMA((MOE_UP_SLOTS,)), pltpu.SemaphoreType.DMA((MOE_DN_SLOTS,))])
    return pl.pallas_call(
        kern, grid_spec=grid_spec,
        out_shape=jax.ShapeDtypeStruct((TOP_K * n, d), F32),
        compiler_params=_cp(("arbitrary",)), name="moe_ffn",
    )(e_i, start, cnt, tok, dst, xn, wg, wu, wd)


def moe_layer(x, g, wrg, brg, wre, bre, wg, wu, wd, layer):
    n_groups = wrg.shape[1]
    n_exp = wre.shape[1]
    d = x.shape[1]
    wr = jnp.zeros((d, LANES), BF16).at[:, :n_groups].set(wrg.astype(BF16))
    wr = wr.at[:, n_groups:n_groups + n_exp].set(wre.astype(BF16))
    br = jnp.zeros((1, LANES), F32).at[0, :n_groups].set(brg).at[0, n_groups:n_groups + n_exp].set(bre)
    xn, eid, wts = router(x, g, wr, br, n_groups, n_exp // n_groups)
    o = moe_ffn(xn, eid[:, :TOP_K], wg, wu, wd, layer)
    return o, wts


def kernel(x_prompt, x_sample, state_conv_a, state_conv_b, state_lru_h, state_gla, norm_mix, norm_ffn,
           norm_final, w_in_even, conv_a_w, conv_b_w, conv_b_b, lru_wa, lru_ba, lru_wx, lru_bx, lru_lambda,
           w_out_even, w_in_odd, gla_wa2, gla_ba, gla_norm, w_out_odd, router_group_w, router_group_b,
           router_expert_w, router_expert_b, moe_w_gate, moe_w_up, moe_w_down):
    pb, pt, d = x_prompt.shape
    sb, st_, _ = x_sample.shape
    assert st_ == 1
    n_p = pb * pt
    n = n_p + sb
    depth = norm_mix.shape[0]
    w = state_lru_h.shape[-1]
    n_h, dk, dv = state_gla.shape[2:]

    new = {k: [] for k in ("pa", "pb", "ph", "ps", "sa", "sb", "sh", "ss")}
    pending = None

    for l in range(depth):
        e = l // 2
        if pending is None:
            x, xn = stack_norm(x_prompt.reshape(n_p, d), x_sample.reshape(sb, d), norm_mix[l])
        else:
            x, xn = combine_norm(x, pending[0], pending[1], norm_mix[l])
        if l % 2 == 0:
            z = matmul(xn, w_in_even[e].astype(BF16))
            wai = jnp.concatenate([lru_wa[e], lru_wx[e]], axis=-1).astype(BF16)
            mix = (conv_a_w[e], conv_b_w[e], conv_b_b[e], wai, lru_ba[e], lru_bx[e], lru_lambda[e])
            y_pr, p_a, p_b, p_h = even_prompt(z, pb, pt, *mix)
            ka, kb = conv_a_w.shape[1], conv_b_w.shape[1]
            y_sm, s_a, s_b, s_h = even_sample(z, n_p, state_conv_a[e].reshape(sb, (ka - 1) * w),
                                              state_conv_b[e].reshape(sb, (kb - 1) * w), state_lru_h[e], *mix)
            new["pa"].append(p_a)
            new["pb"].append(p_b)
            new["ph"].append(p_h.reshape(pb, w))
            new["sa"].append(s_a.reshape(sb, ka - 1, w))
            new["sb"].append(s_b.reshape(sb, kb - 1, w))
            new["sh"].append(s_h)
            w_out = w_out_even[e].astype(BF16)
        else:
            n_main = 2 * n_h * dk + 2 * n_h * dv
            rank = w_in_odd.shape[2] - n_main
            w_in = w_in_odd[e].astype(BF16)
            z = matmul(xn, w_in, n=n_main)
            w_gate = jnp.zeros((d, LANES), BF16).at[:, :rank].set(w_in[:, n_main:])
            zg = matmul(xn, w_gate)
            wa2p = jnp.zeros((LANES, n_h * dk), BF16).at[:rank].set(gla_wa2[e].astype(BF16))
            y_pr, p_s = gla_prompt(z, zg, pb, pt, n_h, dk, dv, wa2p, gla_ba[e], gla_norm[e])
            y_sm, s_s = gla_sample(z, zg, n_p, state_gla[e], wa2p, gla_ba[e], gla_norm[e])
            new["ps"].append(p_s)
            new["ss"].append(s_s)
            w_out = w_out_odd[e].astype(BF16)
        x = matmul_add_rows(y_pr, w_out, x, 0)
        x = matmul_add_rows(y_sm, w_out, x, n_p)
        pending = moe_layer(x, norm_ffn[l], router_group_w[l], router_group_b[l], router_expert_w[l],
                            router_expert_b[l], moe_w_gate, moe_w_up, moe_w_down, l)

    y_p = combine_final(x, pending[0], pending[1], norm_final, 0, n_p).reshape(pb, pt, d)
    y_s = combine_final(x, pending[0], pending[1], norm_final, n_p, sb).reshape(sb, 1, d)
    stack = lambda k: jnp.stack(new[k])
    return (y_p, y_s, stack("pa"), stack("pb"), stack("ph"), stack("ps"),
            stack("sa"), stack("sb"), stack("sh"), stack("ss"))
```

```python
import functools

import jax
import jax.numpy as jnp
from jax import lax
from jax.experimental import pallas as pl
from jax.experimental.pallas import tpu as pltpu

F32 = jnp.float32
BF16 = jnp.bfloat16
I32 = jnp.int32

EPS = 1e-6
LRU_C = 8.0
GLA_GATE_NORM = 16.0
TOP_K = 2
LANES = 128
NEG = -1e30
VMEM_LIMIT = 56 * 1024 * 1024


def _cp(sem, vmem=VMEM_LIMIT):
    return pltpu.CompilerParams(dimension_semantics=sem, vmem_limit_bytes=vmem)


def _tile(n, cands):
    for c in cands:
        if n % c == 0:
            return c
    return n


def _sigmoid(x):
    return 0.5 * jnp.tanh(0.5 * x) + 0.5


def _softplus(x):
    return jnp.maximum(x, 0.0) + jnp.log(1.0 + jnp.exp(-jnp.abs(x)))


def _log_sigmoid(x):
    return -_softplus(-x)


def _one_minus_exp2(x):
    t = jnp.tanh(x)
    return (-2.0 * t) / (1.0 - t)


def _gelu_tanh(x):
    c = 0.7978845608028654
    return 0.5 * x * (1.0 + jnp.tanh(c * (x + 0.044715 * x * x * x)))


def _rms(x, g):
    return x * lax.rsqrt(jnp.mean(x * x, axis=-1, keepdims=True) + EPS) * g


def _combine(x_ref, o0_ref, o1_ref, w_ref):
    w = w_ref[...]
    return x_ref[...] + (o0_ref[...] * w[:, 0:1] + o1_ref[...] * w[:, 1:2])


def _combine_norm_kernel(x_ref, o0_ref, o1_ref, w_ref, g_ref, xnew_ref, xn_ref):
    x = _combine(x_ref, o0_ref, o1_ref, w_ref)
    xnew_ref[...] = x
    xn_ref[...] = _rms(x, g_ref[...]).astype(xn_ref.dtype)


def _combine_final_kernel(x_ref, o0_ref, o1_ref, w_ref, g_ref, y_ref):
    y_ref[...] = _rms(_combine(x_ref, o0_ref, o1_ref, w_ref), g_ref[...])


def _router_kernel(x_ref, g_ref, wr_ref, br_ref, xn_ref, eid_ref, wts_ref, *, n_groups, per_group):
    xn = _rms(x_ref[...], g_ref[...])
    xn_ref[...] = xn
    logits = jnp.dot(xn.astype(BF16), wr_ref[...], preferred_element_type=F32) + br_ref[...]
    tm = logits.shape[0]
    n_exp = n_groups * per_group
    lane = lax.broadcasted_iota(I32, (tm, LANES), 1)
    is_g = lane < n_groups
    lg = jnp.where(is_g, logits, NEG)
    mg = jnp.max(lg, axis=1, keepdims=True)
    gidx = jnp.min(jnp.where(is_g & (lg == mg), lane, LANES), axis=1, keepdims=True)
    pg = 1.0 / jnp.sum(jnp.where(is_g, jnp.exp(lg - mg), 0.0), axis=1, keepdims=True)
    lo = n_groups + gidx * per_group
    in_grp = (lane >= lo) & (lane < lo + per_group) & (lane < n_groups + n_exp)
    le = jnp.where(in_grp, logits, NEG)
    m1 = jnp.max(le, axis=1, keepdims=True)
    i1 = jnp.min(jnp.where(in_grp & (le == m1), lane, LANES), axis=1, keepdims=True)
    rest = in_grp & (lane != i1)
    le2 = jnp.where(rest, logits, NEG)
    m2 = jnp.max(le2, axis=1, keepdims=True)
    i2 = jnp.min(jnp.where(rest & (le2 == m2), lane, LANES), axis=1, keepdims=True)
    t = jnp.exp(m2 - m1)
    w1 = pg / (1.0 + t)
    w2 = pg * t / (1.0 + t)
    eid_ref[...] = jnp.where(lane == 0, i1 - n_groups, jnp.where(lane == 1, i2 - n_groups, 0))
    wts_ref[...] = jnp.where(lane == 0, w1, jnp.where(lane == 1, w2, 0.0))


def _row_spec(tm, d, off=0):
    return pl.BlockSpec((tm, d), lambda i: (i + off, 0))


def _const_spec(shape):
    return pl.BlockSpec(shape, lambda *_: tuple(0 for _ in shape))


def _stack_norm_kernel(xp_ref, xs_ref, g_ref, x_ref, xn_ref, *, nbp):
    def emit(src_ref):
        x = src_ref[...]
        x_ref[...] = x
        xn_ref[...] = _rms(x, g_ref[...]).astype(xn_ref.dtype)

    pl.when(pl.program_id(0) < nbp)(lambda: emit(xp_ref))
    pl.when(pl.program_id(0) >= nbp)(lambda: emit(xs_ref))


def stack_norm(xp, xs, g):
    (n_p, d), n_s = xp.shape, xs.shape[0]
    tm = next(c for c in (128, 64, 32, 16, 8) if n_p % c == 0 and n_s % c == 0)
    nbp, nbs = n_p // tm, n_s // tm
    kern = functools.partial(_stack_norm_kernel, nbp=nbp)
    return pl.pallas_call(
        kern, grid=(nbp + nbs,),
        in_specs=[pl.BlockSpec((tm, d), lambda i: (jnp.minimum(i, nbp - 1), 0)),
                  pl.BlockSpec((tm, d), lambda i: (jnp.maximum(i - nbp, 0), 0)),
                  _const_spec((1, d))],
        out_specs=[_row_spec(tm, d), _row_spec(tm, d)],
        out_shape=[jax.ShapeDtypeStruct((n_p + n_s, d), F32), jax.ShapeDtypeStruct((n_p + n_s, d), BF16)],
        compiler_params=_cp(("parallel",)), name="stack_norm",
    )(xp, xs, g.reshape(1, d))


def combine_norm(x, o, w, g):
    n, d = x.shape
    tm = _tile(n, (128, 64, 32, 16, 8))
    nb = n // tm
    return pl.pallas_call(
        _combine_norm_kernel, grid=(nb,),
        in_specs=[_row_spec(tm, d), _row_spec(tm, d), _row_spec(tm, d, nb), _row_spec(tm, LANES),
                  _const_spec((1, d))],
        out_specs=[_row_spec(tm, d), _row_spec(tm, d)],
        out_shape=[jax.ShapeDtypeStruct((n, d), F32), jax.ShapeDtypeStruct((n, d), BF16)],
        compiler_params=_cp(("parallel",)), name="combine_norm",
    )(x, o, o, w, g.reshape(1, d))


def combine_final(x, o, w, g, row0, rows):
    n, d = x.shape
    tm = next(c for c in (128, 64, 32, 16, 8) if rows % c == 0 and row0 % c == 0 and n % c == 0)
    off = row0 // tm
    return pl.pallas_call(
        _combine_final_kernel, grid=(rows // tm,),
        in_specs=[_row_spec(tm, d, off), _row_spec(tm, d, off), _row_spec(tm, d, off + n // tm),
                  _row_spec(tm, LANES, off), _const_spec((1, d))],
        out_specs=_row_spec(tm, d),
        out_shape=jax.ShapeDtypeStruct((rows, d), F32),
        compiler_params=_cp(("parallel",)), name="combine_final",
    )(x, o, o, w, g.reshape(1, d))


def router(x, g, wr, br, n_groups, per_group):
    n, d = x.shape
    tm = _tile(n, (320, 256, 128, 64, 32, 16, 8))
    kern = functools.partial(_router_kernel, n_groups=n_groups, per_group=per_group)
    return pl.pallas_call(
        kern, grid=(n // tm,),
        in_specs=[_row_spec(tm, d), _const_spec((1, d)), _const_spec((d, LANES)), _const_spec((1, LANES))],
        out_specs=[_row_spec(tm, d), _row_spec(tm, LANES), _row_spec(tm, LANES)],
        out_shape=[jax.ShapeDtypeStruct((n, d), F32), jax.ShapeDtypeStruct((n, LANES), I32),
                   jax.ShapeDtypeStruct((n, LANES), F32)],
        compiler_params=_cp(("parallel",)), name="router",
    )(x, g.reshape(1, d), wr, br)


def _mm_kernel(a_ref, b_ref, o_ref):
    o_ref[...] = jnp.dot(a_ref[...], b_ref[...], preferred_element_type=F32).astype(o_ref.dtype)


def _mm_res_kernel(a_ref, b_ref, r_ref, o_ref):
    o_ref[...] = r_ref[...] + jnp.dot(a_ref[...], b_ref[...], preferred_element_type=F32)


MM_TM = (1024, 832, 640, 512, 256, 128, 64, 32, 16)
MM_TN = (1024, 512, 256, 128)


def matmul(a, b, out_dtype=F32, n=None):
    m, k = a.shape
    n = b.shape[1] if n is None else n
    tm, tn = _tile(m, MM_TM), _tile(n, MM_TN)
    return pl.pallas_call(
        _mm_kernel, grid=(m // tm, n // tn),
        in_specs=[pl.BlockSpec((tm, k), lambda i, j: (i, 0)), pl.BlockSpec((k, tn), lambda i, j: (0, j))],
        out_specs=pl.BlockSpec((tm, tn), lambda i, j: (i, j)),
        out_shape=jax.ShapeDtypeStruct((m, n), out_dtype),
        compiler_params=_cp(("parallel", "parallel")), name="matmul",
    )(a, b)


def _mm_nt_kernel(a_ref, bt_ref, o_ref):
    o_ref[...] = lax.dot_general(a_ref[...], bt_ref[...], (((1,), (1,)), ((), ())),
                                 preferred_element_type=F32).astype(o_ref.dtype)


def matmul_nt(a, bt, out_dtype=F32, n=None):
    m, k = a.shape
    n = bt.shape[0] if n is None else n
    tm, tn = _tile(m, MM_TM), _tile(n, MM_TN)
    return pl.pallas_call(
        _mm_nt_kernel, grid=(m // tm, n // tn),
        in_specs=[pl.BlockSpec((tm, k), lambda i, j: (i, 0)), pl.BlockSpec((tn, k), lambda i, j: (j, 0))],
        out_specs=pl.BlockSpec((tm, tn), lambda i, j: (i, j)),
        out_shape=jax.ShapeDtypeStruct((m, n), out_dtype),
        compiler_params=_cp(("parallel", "parallel")), name="matmul_nt",
    )(a, bt)


def matmul_add_rows(a, b, x, row0):
    m, k = a.shape
    _, n = b.shape
    tm, tn = _tile(m, MM_TM), _tile(n, MM_TN)
    assert row0 % tm == 0
    off = row0 // tm
    xs = pl.BlockSpec((tm, tn), lambda i, j: (i + off, j))
    return pl.pallas_call(
        _mm_res_kernel, grid=(m // tm, n // tn),
        in_specs=[pl.BlockSpec((tm, k), lambda i, j: (i, 0)), pl.BlockSpec((k, tn), lambda i, j: (0, j)), xs],
        out_specs=xs,
        out_shape=jax.ShapeDtypeStruct(x.shape, F32),
        input_output_aliases={2: 0},
        compiler_params=_cp(("parallel", "parallel")), name="matmul_add_rows",
    )(a, b, x)


def _lru_gates(xc, wai_ref, ba_ref, bx_ref, lam_ref, a_out, b_out):
    n_heads, hd, _ = wai_ref.shape
    for h in range(n_heads):
        sl = slice(h * hd, (h + 1) * hd)
        xh = xc[:, sl]
        ri = jnp.dot(xh.astype(BF16), wai_ref[h], preferred_element_type=F32)
        r = _sigmoid(ri[:, :hd] + ba_ref[:, sl])
        i = _sigmoid(ri[:, hd:] + bx_ref[:, sl])
        log_a = (-LRU_C) * r * _softplus(-lam_ref[:, sl])
        a_out[:, sl] = jnp.exp(log_a)
        b_out[:, sl] = jnp.sqrt(_one_minus_exp2(log_a)) * (i * xh)


def _even_prompt_kernel(gb_ref, gc_ref, xa_ref, xr_ref, gr_ref, caw_ref, cbw_ref, cbb_ref, wai_ref,
                        ba_ref, bx_ref, lam_ref, y_ref, pa_ref, pb_ref, ph_ref,
                        ubuf, rbuf, hcar, abuf, bbuf, hbuf):
    t = pl.program_id(1)
    tt, w = gb_ref.shape
    ka, kb = caw_ref.shape[0], cbw_ref.shape[0]

    @pl.when(t == 0)
    def _():
        ubuf[0:8, :] = jnp.zeros((8, w), F32)
        rbuf[0:8, :] = jnp.zeros((8, w), F32)
        hcar[...] = jnp.zeros_like(hcar)

    u = gc_ref[...] * xa_ref[...]
    ubuf[8:8 + tt, :] = u
    v = caw_ref[ka - 1:ka, :] * u
    for k in range(1, ka):
        v = v + caw_ref[ka - 1 - k:ka - k, :] * ubuf[8 - k:8 - k + tt, :]
    y_ref[:, 0:w] = (gb_ref[...] * v).astype(y_ref.dtype)

    xr = xr_ref[...]
    rbuf[8:8 + tt, :] = xr
    xc = cbw_ref[kb - 1:kb, :] * xr + cbb_ref[...]
    for k in range(1, kb):
        xc = xc + cbw_ref[kb - 1 - k:kb - k, :] * rbuf[8 - k:8 - k + tt, :]
    _lru_gates(xc, wai_ref, ba_ref, bx_ref, lam_ref, abuf, bbuf)

    def step(i, h):
        h = abuf[pl.ds(i, 1), :] * h + bbuf[pl.ds(i, 1), :]
        hbuf[pl.ds(i, 1), :] = h
        return h

    h = lax.fori_loop(0, tt, step, hcar[0:1, :], unroll=8)
    hcar[0:1, :] = h
    y_ref[:, w:2 * w] = (_gelu_tanh(gr_ref[...]) * hbuf[...]).astype(y_ref.dtype)

    pa_ref[0] = ubuf[8 + tt - (ka - 1):8 + tt, :]
    pb_ref[0] = rbuf[8 + tt - (kb - 1):8 + tt, :]
    ph_ref[0] = h
    ubuf[0:8, :] = ubuf[tt:tt + 8, :]
    rbuf[0:8, :] = rbuf[tt:tt + 8, :]


def even_prompt(z, n_b, n_t, caw, cbw, cbb, wai, ba, bx, lam):
    n = n_b * n_t
    w = z.shape[1] // 5
    tt = _tile(n_t, (128, 64, 32, 16, 8))
    nt = n_t // tt
    ka, kb = caw.shape[0], cbw.shape[0]

    def zs(c):
        return pl.BlockSpec((tt, w), lambda b, t: (b * nt + t, c))

    return pl.pallas_call(
        _even_prompt_kernel, grid=(n_b, nt),
        in_specs=[zs(0), zs(1), zs(2), zs(3), zs(4), _const_spec((ka, w)), _const_spec((kb, w)),
                  _const_spec((1, w)), _const_spec(wai.shape), _const_spec((1, w)), _const_spec((1, w)),
                  _const_spec((1, w))],
        out_specs=[pl.BlockSpec((tt, 2 * w), lambda b, t: (b * nt + t, 0)),
                   pl.BlockSpec((1, ka - 1, w), lambda b, t: (b, 0, 0)),
                   pl.BlockSpec((1, kb - 1, w), lambda b, t: (b, 0, 0)),
                   pl.BlockSpec((1, 1, w), lambda b, t: (b, 0, 0))],
        out_shape=[jax.ShapeDtypeStruct((n, 2 * w), BF16),
                   jax.ShapeDtypeStruct((n_b, ka - 1, w), F32),
                   jax.ShapeDtypeStruct((n_b, kb - 1, w), F32),
                   jax.ShapeDtypeStruct((n_b, 1, w), F32)],
        scratch_shapes=[pltpu.VMEM((8 + tt, w), F32), pltpu.VMEM((8 + tt, w), F32), pltpu.VMEM((8, w), F32),
                        pltpu.VMEM((tt, w), F32), pltpu.VMEM((tt, w), F32), pltpu.VMEM((tt, w), F32)],
        compiler_params=_cp(("parallel", "arbitrary")), name="even_prompt",
    )(z, z, z, z, z, caw, cbw, cbb.reshape(1, w), wai, ba.reshape(1, w), bx.reshape(1, w), lam.reshape(1, w))


def _even_sample_kernel(gb_ref, gc_ref, xa_ref, xr_ref, gr_ref, sa_ref, sb_ref, sh_ref, caw_ref, cbw_ref,
                        cbb_ref, wai_ref, ba_ref, bx_ref, lam_ref, y_ref, na_ref, nb_ref, nh_ref,
                        abuf, bbuf):
    w = gb_ref.shape[1]
    ka, kb = caw_ref.shape[0], cbw_ref.shape[0]
    u = gc_ref[...] * xa_ref[...]
    v = caw_ref[ka - 1:ka, :] * u
    for k in range(1, ka):
        v = v + caw_ref[ka - 1 - k:ka - k, :] * sa_ref[:, (ka - 1 - k) * w:(ka - k) * w]
    y_ref[:, 0:w] = (gb_ref[...] * v).astype(y_ref.dtype)
    for k in range(ka - 2):
        na_ref[:, k * w:(k + 1) * w] = sa_ref[:, (k + 1) * w:(k + 2) * w]
    na_ref[:, (ka - 2) * w:(ka - 1) * w] = u

    xr = xr_ref[...]
    xc = cbw_ref[kb - 1:kb, :] * xr + cbb_ref[...]
    for k in range(1, kb):
        xc = xc + cbw_ref[kb - 1 - k:kb - k, :] * sb_ref[:, (kb - 1 - k) * w:(kb - k) * w]
    for k in range(kb - 2):
        nb_ref[:, k * w:(k + 1) * w] = sb_ref[:, (k + 1) * w:(k + 2) * w]
    nb_ref[:, (kb - 2) * w:(kb - 1) * w] = xr

    _lru_gates(xc, wai_ref, ba_ref, bx_ref, lam_ref, abuf, bbuf)
    h = abuf[...] * sh_ref[...] + bbuf[...]
    nh_ref[...] = h
    y_ref[:, w:2 * w] = (_gelu_tanh(gr_ref[...]) * h).astype(y_ref.dtype)


def even_sample(z, row0, sa, sb, sh, caw, cbw, cbb, wai, ba, bx, lam):
    s, w = sh.shape
    ka, kb = caw.shape[0], cbw.shape[0]
    assert row0 % s == 0
    rb = row0 // s

    def zs(c):
        return pl.BlockSpec((s, w), lambda i: (rb, c))

    full = lambda shape: pl.BlockSpec(shape, lambda i: tuple(0 for _ in shape))
    return pl.pallas_call(
        _even_sample_kernel, grid=(1,),
        in_specs=[zs(0), zs(1), zs(2), zs(3), zs(4), full(sa.shape), full(sb.shape), full(sh.shape),
                  full((ka, w)), full((kb, w)), full((1, w)), full(wai.shape), full((1, w)), full((1, w)),
                  full((1, w))],
        out_specs=[full((s, 2 * w)), full(sa.shape), full(sb.shape), full(sh.shape)],
        out_shape=[jax.ShapeDtypeStruct((s, 2 * w), BF16), jax.ShapeDtypeStruct(sa.shape, F32),
                   jax.ShapeDtypeStruct(sb.shape, F32), jax.ShapeDtypeStruct(sh.shape, F32)],
        scratch_shapes=[pltpu.VMEM((s, w), F32), pltpu.VMEM((s, w), F32)],
        compiler_params=_cp(("arbitrary",)), name="even_sample",
    )(z, z, z, z, z, sa, sb, sh, caw, cbw, cbb.reshape(1, w), wai, ba.reshape(1, w), bx.reshape(1, w),
      lam.reshape(1, w))


def _gla_log_decay(gr_ref, wa2_ref, ba_ref):
    pre = jnp.dot(gr_ref[...].astype(BF16), wa2_ref[...], preferred_element_type=F32) + ba_ref[...]
    return _log_sigmoid(pre) * (1.0 / GLA_GATE_NORM)


def _head_norm_gate(o, gn_ref, g):
    o = o * lax.rsqrt(jnp.mean(o * o, axis=-1, keepdims=True) + EPS) * gn_ref[...]
    return o * (g * _sigmoid(g))


def _cumsum_rows(tri16, x):
    hi = x.astype(BF16)
    r1 = x - hi.astype(F32)
    mid = r1.astype(BF16)
    lo = (r1 - mid.astype(F32)).astype(BF16)
    dot = lambda t: jnp.dot(tri16, t, preferred_element_type=F32)
    return dot(hi) + dot(mid) + dot(lo)


GLA_SUB = 16
GLA_SAFE_LOG = 60.0


def _gla_rows(r0, cs, q_ref, k_ref, v_ref, g_ref, gn_ref, o_ref, gabuf, st):
    n_h, dv, dk = st.shape
    rows = pl.ds(r0, cs)
    row = lax.broadcasted_iota(I32, (cs, cs), 0)
    col = lax.broadcasted_iota(I32, (cs, cs), 1)
    causal = row >= col
    b = _cumsum_rows(causal.astype(BF16), gabuf[rows, :])
    b_last = b[cs - 1:cs, :]
    k = k_ref[rows, :]
    qe_all = (q_ref[rows, :] * (dk ** -0.5) * jnp.exp(b)).astype(BF16)
    ke_all = (k * jnp.exp(-b)).astype(BF16)
    kd_all = (k * jnp.exp(b_last - b)).astype(BF16)
    decay = jnp.exp(b_last)
    nt = (((1,), (1,)), ((), ()))
    tn = (((0,), (0,)), ((), ()))
    for h in range(n_h):
        sk, sv = slice(h * dk, (h + 1) * dk), slice(h * dv, (h + 1) * dv)
        qe, ke, kd = qe_all[:, sk], ke_all[:, sk], kd_all[:, sk]
        v16 = v_ref[rows, sv].astype(BF16)
        att = lax.dot_general(qe, ke, nt, preferred_element_type=F32)
        att = jnp.where(causal, att, 0.0).astype(BF16)
        s_old = st[h]
        o = lax.dot_general(qe, s_old.astype(BF16), nt, preferred_element_type=F32)
        o = o + jnp.dot(att, v16, preferred_element_type=F32)
        st[h] = s_old * decay[:, sk] + lax.dot_general(v16, kd, tn, preferred_element_type=F32)
        o_ref[rows, sv] = _head_norm_gate(o, gn_ref, g_ref[rows, sv]).astype(o_ref.dtype)


def _gla_prompt_kernel(q_ref, k_ref, v_ref, g_ref, gr_ref, wa2_ref, ba_ref, gn_ref, o_ref, s_ref, gabuf, st):
    c = pl.program_id(1)
    cs = q_ref.shape[0]

    @pl.when(c == 0)
    def _():
        st[...] = jnp.zeros_like(st)

    ga = _gla_log_decay(gr_ref, wa2_ref, ba_ref)
    gabuf[...] = ga
    one_piece = jnp.min(jnp.sum(ga, axis=0, keepdims=True)) > -GLA_SAFE_LOG
    args = (q_ref, k_ref, v_ref, g_ref, gn_ref, o_ref, gabuf, st)

    @pl.when(one_piece)
    def _():
        _gla_rows(0, cs, *args)

    @pl.when(jnp.logical_not(one_piece))
    def _():
        def sub(j, carry):
            _gla_rows(pl.multiple_of(j * GLA_SUB, GLA_SUB), GLA_SUB, *args)
            return carry
        lax.fori_loop(0, cs // GLA_SUB, sub, 0)

    @pl.when(c == pl.num_programs(1) - 1)
    def _():
        for h in range(st.shape[0]):
            s_ref[0, h] = st[h].T


def gla_prompt(z, zg, n_b, n_t, n_h, dk, dv, wa2p, ba, gnorm):
    n = n_b * n_t
    cs = _tile(n_t, (128, 64, 32, 16))
    nc = n_t // cs
    dkt, dvt = n_h * dk, n_h * dv
    assert (2 * dkt) % dvt == 0 and cs % GLA_SUB == 0
    vb0 = (2 * dkt) // dvt
    row = lambda b, c: b * nc + c
    return pl.pallas_call(
        _gla_prompt_kernel, grid=(n_b, nc),
        in_specs=[pl.BlockSpec((cs, dkt), lambda b, c: (row(b, c), 0)),
                  pl.BlockSpec((cs, dkt), lambda b, c: (row(b, c), 1)),
                  pl.BlockSpec((cs, dvt), lambda b, c: (row(b, c), vb0)),
                  pl.BlockSpec((cs, dvt), lambda b, c: (row(b, c), vb0 + 1)),
                  pl.BlockSpec((cs, LANES), lambda b, c: (row(b, c), 0)),
                  _const_spec((LANES, dkt)), _const_spec((1, dkt)), _const_spec((1, dv))],
        out_specs=[pl.BlockSpec((cs, dvt), lambda b, c: (row(b, c), 0)),
                   pl.BlockSpec((1, n_h, dk, dv), lambda b, c: (b, 0, 0, 0))],
        out_shape=[jax.ShapeDtypeStruct((n, dvt), BF16),
                   jax.ShapeDtypeStruct((n_b, n_h, dk, dv), F32)],
        scratch_shapes=[pltpu.VMEM((cs, dkt), F32), pltpu.VMEM((n_h, dv, dk), F32)],
        compiler_params=_cp(("parallel", "arbitrary")), name="gla_prompt",
    )(z, z, z, z, zg, wa2p, ba.reshape(1, -1), gnorm.reshape(1, dv))


def _gla_sample_kernel(q_ref, k_ref, v_ref, g_ref, gr_ref, wa2_ref, ba_ref, gn_ref, s0_ref,
                       o_ref, s_ref, ta, tk, tq):
    i = pl.program_id(1)
    nblk, dk, bb = ta.shape

    @pl.when(i == 0)
    def _():
        alpha_t = jnp.exp(_gla_log_decay(gr_ref, wa2_ref, ba_ref)).T
        k_t = k_ref[...].T
        q_t = (q_ref[...] * (dk ** -0.5)).T
        for j in range(nblk):
            ta[j] = alpha_t[:, j * bb:(j + 1) * bb]
            tk[j] = k_t[:, j * bb:(j + 1) * bb]
            tq[j] = q_t[:, j * bb:(j + 1) * bb]

    a3, k3, q3 = ta[i], tk[i], tq[i]
    v = v_ref[...]
    rows = []
    for j in range(bb):
        s_new = s0_ref[j, 0] * a3[:, j:j + 1] + k3[:, j:j + 1] * v[j:j + 1, :]
        s_ref[j, 0] = s_new
        rows.append(jnp.sum(q3[:, j:j + 1] * s_new, axis=0, keepdims=True))
    o = jnp.concatenate(rows, axis=0)
    o_ref[...] = _head_norm_gate(o, gn_ref, g_ref[...]).astype(o_ref.dtype)


def gla_sample(z, zg, row0, s0, wa2p, ba, gnorm):
    s, n_h, dk, dv = s0.shape
    bb = _tile(s, (16, 8))
    nblk = s // bb
    assert row0 % s == 0 and (2 * n_h * dk) % dv == 0
    rs, rb = row0 // s, row0 // bb
    kb0 = n_h
    vb0 = (2 * n_h * dk) // dv
    gb0 = vb0 + n_h
    return pl.pallas_call(
        _gla_sample_kernel, grid=(n_h, nblk),
        in_specs=[pl.BlockSpec((s, dk), lambda h, i: (rs, h)),
                  pl.BlockSpec((s, dk), lambda h, i: (rs, kb0 + h)),
                  pl.BlockSpec((bb, dv), lambda h, i: (rb + i, vb0 + h)),
                  pl.BlockSpec((bb, dv), lambda h, i: (rb + i, gb0 + h)),
                  pl.BlockSpec((s, LANES), lambda h, i: (rs, 0)),
                  pl.BlockSpec((LANES, dk), lambda h, i: (0, h)),
                  pl.BlockSpec((1, dk), lambda h, i: (0, h)),
                  pl.BlockSpec((1, dv), lambda h, i: (0, 0)),
                  pl.BlockSpec((bb, 1, dk, dv), lambda h, i: (i, h, 0, 0))],
        out_specs=[pl.BlockSpec((bb, dv), lambda h, i: (i, h)),
                   pl.BlockSpec((bb, 1, dk, dv), lambda h, i: (i, h, 0, 0))],
        out_shape=[jax.ShapeDtypeStruct((s, n_h * dv), BF16), jax.ShapeDtypeStruct(s0.shape, F32)],
        scratch_shapes=[pltpu.VMEM((nblk, dk, bb), F32)] * 3,
        compiler_params=_cp(("arbitrary", "arbitrary")), name="gla_sample",
    )(z, z, z, z, zg, wa2p, ba.reshape(1, -1), gnorm.reshape(1, dv), s0)


MOE_SHORT_TILE = 128
MOE_AHEAD = 4
MOE_ROW_UNROLL = 8
MOE_WEIGHT_DMA_PRIORITY = 1


def _moe_kernel(ie_ref, istart_ref, icnt_ref, tok_ref, dst_ref, xn_hbm, wg_hbm, wu_hbm, wd_hbm, o_hbm,
                xstage, obuf, xb16, hbuf, ring_up, ring_dn, wg16, wu16, wd16,
                sem_x, sem_o, sem_up, sem_dn, *, layer, tiles):
    it = pl.program_id(0)
    last = pl.num_programs(0) - 1
    e, start, cnt = ie_ref[it], istart_ref[it], icnt_ref[it]
    nxt = jnp.minimum(it + 1, last)
    has_next = (it < last) & (icnt_ref[nxt] > 0)
    e_next, start_next, cnt_next = ie_ref[nxt], istart_ref[nxt], icnt_ref[nxt]
    prv = jnp.maximum(it - 1, 0)
    start_prev, cnt_prev = istart_ref[prv], jnp.where(it > 0, icnt_ref[prv], 0)
    fc, dc = wg16.shape[1], wd16.shape[1]
    nf, nd = hbuf.shape[1] // fc, obuf.shape[1] // dc
    n_chunks = 2 * nf + nd
    up_slots, dn_slots = ring_up.shape[0], ring_dn.shape[0]
    ahead = dn_slots
    big, tails = tiles[0], tiles[1:]

    def chunk_copy(k, ex):
        if k < 2 * nf:
            w_hbm = wg_hbm if k % 2 == 0 else wu_hbm
            slot = k % up_slots
            return pltpu.make_async_copy(w_hbm.at[layer, ex, :, pl.ds((k // 2) * fc, fc)], ring_up.at[slot],
                                         sem_up.at[slot])
        c = k - 2 * nf
        slot = c % dn_slots
        return pltpu.make_async_copy(wd_hbm.at[layer, ex, :, pl.ds(c * dc, dc)], ring_dn.at[slot],
                                     sem_dn.at[slot])

    def prefetch(ks):
        for k in ks:
            if k < n_chunks:
                chunk_copy(k, e).start(priority=MOE_WEIGHT_DMA_PRIORITY)
            else:
                @pl.when(has_next)
                def _(k=k):
                    chunk_copy(k - n_chunks, e_next).start(priority=MOE_WEIGHT_DMA_PRIORITY)

    def gather_copy(r, base):
        return pltpu.make_async_copy(xn_hbm.at[pl.ds(tok_ref[base + r], 1), :], xstage.at[pl.ds(r, 1), :], sem_x)

    def scatter_copy(r, base):
        return pltpu.make_async_copy(obuf.at[pl.ds(r, 1), :], o_hbm.at[pl.ds(dst_ref[base + r], 1), :], sem_o)

    def for_rows(n, fn):
        n_grp = lax.shift_right_logical(n, MOE_ROW_UNROLL.bit_length() - 1)

        def group(g, c):
            for u in range(MOE_ROW_UNROLL):
                fn(pl.multiple_of(g * MOE_ROW_UNROLL, MOE_ROW_UNROLL) + u)
            return c
        lax.fori_loop(0, n_grp, group, 0)

        def single(r, c):
            fn(r)
            return c
        lax.fori_loop(n_grp * MOE_ROW_UNROLL, n, single, 0)

    def for_tiles(fn):
        rem = cnt % big
        n_big = cnt // big + (rem > (tails[0] if tails else 0)).astype(I32)

        def body(j, c):
            fn(pl.multiple_of(j * big, big), big)
            return c
        lax.fori_loop(0, n_big, body, 0)

        for t, t_below in zip(tails, tails[1:] + (0,)):
            @pl.when((rem > t_below) & (rem <= t))
            def _(t=t):
                fn(pl.multiple_of(n_big * big, big), t)

    @pl.when(it == 0)
    def _():
        xstage[...] = jnp.zeros_like(xstage)

    @pl.when((it == 0) & (cnt > 0))
    def _():
        prefetch(range(ahead))
        for_rows(cnt, lambda r: gather_copy(r, start).start())

    @pl.when(cnt > 0)
    def _():
        for_rows(cnt, lambda r: gather_copy(r, start).wait())

        def cast(r0, t):
            xb16[pl.ds(r0, t), :] = xstage[pl.ds(r0, t), :].astype(BF16)
        for_tiles(cast)

        @pl.when(has_next)
        def _():
            for_rows(cnt_next, lambda r: gather_copy(r, start_next).start())

        for c in range(nf):
            chunk_copy(2 * c, e).wait()
            wg16[...] = ring_up[(2 * c) % up_slots].astype(BF16)
            chunk_copy(2 * c + 1, e).wait()
            wu16[...] = ring_up[(2 * c + 1) % up_slots].astype(BF16)
            prefetch((2 * c + ahead, 2 * c + ahead + 1))

            def up(r0, t, c=c):
                x = xb16[pl.ds(r0, t), :]
                g = jnp.dot(x, wg16[...], preferred_element_type=F32)
                u = jnp.dot(x, wu16[...], preferred_element_type=F32)
                hbuf[pl.ds(r0, t), c * fc:(c + 1) * fc] = (g * _sigmoid(g) * u).astype(BF16)
            for_tiles(up)

        for_rows(cnt_prev, lambda r: scatter_copy(r, start_prev).wait())

        for c in range(nd):
            k = 2 * nf + c
            chunk_copy(k, e).wait()
            wd16[...] = ring_dn[c % dn_slots].astype(BF16)
            prefetch((k + ahead,))

            def down(r0, t, c=c):
                obuf[pl.ds(r0, t), c * dc:(c + 1) * dc] = jnp.dot(hbuf[pl.ds(r0, t), :], wd16[...],
                                                                 preferred_element_type=F32)
            for_tiles(down)

        for_rows(cnt, lambda r: scatter_copy(r, start).start())

        @pl.when(jnp.logical_not(has_next))
        def _():
            for_rows(cnt, lambda r: scatter_copy(r, start).wait())


def moe_route_tables(eid, n_exp, rows):
    n = eid.shape[0]
    flat = eid.reshape(-1)
    na = flat.shape[0]
    order = jnp.argsort(flat).astype(I32)
    counts = jnp.bincount(flat, length=n_exp).astype(I32)
    gstart = jnp.cumsum(counts) - counts
    n_it = (counts + rows - 1) // rows
    it_cum = jnp.cumsum(n_it)
    it_base = it_cum - n_it
    total = it_cum[-1]
    n_items = n_exp + na // rows
    ii = jnp.arange(n_items, dtype=I32)
    e_i = jnp.minimum(jnp.searchsorted(it_cum, ii, side="right").astype(I32), n_exp - 1)
    valid = ii < total
    e_i = jnp.where(valid, e_i, e_i[jnp.maximum(total - 1, 0)])
    j = ii - it_base[e_i]
    start = jnp.where(valid, gstart[e_i] + j * rows, 0).astype(I32)
    cnt = jnp.where(valid, jnp.minimum(rows, counts[e_i] - j * rows), 0).astype(I32)
    tok = order // TOP_K
    dst = (order % TOP_K) * n + tok
    return e_i, start, cnt, tok, dst


def _moe_tiles(n_assign, n_exp):
    mean = n_assign / n_exp
    big = max(MOE_SHORT_TILE, -(-int(mean + 2.0 * mean ** 0.5) // 32) * 32)
    return (big, MOE_SHORT_TILE) if big > MOE_SHORT_TILE else (big,)


def moe_ffn(xn, eid, wg, wu, wd, layer):
    n, d = xn.shape
    _, n_exp, _, f = wg.shape
    tiles = _moe_tiles(TOP_K * n, n_exp)
    rows = tiles[0]
    fc = _tile(f, (256, 128))
    dc = _tile(d, (1024, 512, 256, 128))
    nf, nd = f // fc, d // dc
    ahead = min(MOE_AHEAD, 2 * nf + nd, nd + 1)
    up_slots, dn_slots = ahead + 1, ahead
    e_i, start, cnt, tok, dst = moe_route_tables(eid, n_exp, rows)
    n_items = e_i.shape[0]
    kern = functools.partial(_moe_kernel, layer=layer, tiles=tiles)
    any_spec = pl.BlockSpec(memory_space=pl.ANY)
    grid_spec = pltpu.PrefetchScalarGridSpec(
        num_scalar_prefetch=5, grid=(n_items,),
        in_specs=[any_spec, any_spec, any_spec, any_spec],
        out_specs=any_spec,
        scratch_shapes=[pltpu.VMEM((rows, d), F32), pltpu.VMEM((rows, d), F32), pltpu.VMEM((rows, d), BF16),
                        pltpu.VMEM((rows, f), BF16),
                        pltpu.VMEM((up_slots, d, fc), F32), pltpu.VMEM((dn_slots, f, dc), F32),
                        pltpu.VMEM((d, fc), BF16), pltpu.VMEM((d, fc), BF16), pltpu.VMEM((f, dc), BF16),
                        pltpu.SemaphoreType.DMA(()), pltpu.SemaphoreType.DMA(()),
                        pltpu.SemaphoreType.DMA((up_slots,)), pltpu.SemaphoreType.DMA((dn_slots,))])
    return pl.pallas_call(
        kern, grid_spec=grid_spec,
        out_shape=jax.ShapeDtypeStruct((TOP_K * n, d), F32),
        compiler_params=_cp(("arbitrary",)), name="moe_ffn",
    )(e_i, start, cnt, tok, dst, xn, wg, wu, wd)


def moe_layer(x, g, wrg, brg, wre, bre, wg, wu, wd, layer):
    n_groups = wrg.shape[1]
    n_exp = wre.shape[1]
    d = x.shape[1]
    wr = jnp.zeros((d, LANES), BF16).at[:, :n_groups].set(wrg.astype(BF16))
    wr = wr.at[:, n_groups:n_groups + n_exp].set(wre.astype(BF16))
    br = jnp.zeros((1, LANES), F32).at[0, :n_groups].set(brg).at[0, n_groups:n_groups + n_exp].set(bre)
    xn, eid, wts = router(x, g, wr, br, n_groups, n_exp // n_groups)
    o = moe_ffn(xn, eid[:, :TOP_K], wg, wu, wd, layer)
    return o, wts


def kernel(x_prompt, x_sample, state_conv_a, state_conv_b, state_lru_h, state_gla, norm_mix, norm_ffn,
           norm_final, w_in_even, conv_a_w, conv_b_w, conv_b_b, lru_wa, lru_ba, lru_wx, lru_bx, lru_lambda,
           w_out_even, w_in_odd, gla_wa2, gla_ba, gla_norm, w_out_odd, router_group_w, router_group_b,
           router_expert_w, router_expert_b, moe_w_gate, moe_w_up, moe_w_down):
    pb, pt, d = x_prompt.shape
    sb, st_, _ = x_sample.shape
    assert st_ == 1
    n_p = pb * pt
    n = n_p + sb
    depth = norm_mix.shape[0]
    w = state_lru_h.shape[-1]
    n_h, dk, dv = state_gla.shape[2:]

    new = {k: [] for k in ("pa", "pb", "ph", "ps", "sa", "sb", "sh", "ss")}
    pending = None

    for l in range(depth):
        e = l // 2
        if pending is None:
            x, xn = stack_norm(x_prompt.reshape(n_p, d), x_sample.reshape(sb, d), norm_mix[l])
        else:
            x, xn = combine_norm(x, pending[0], pending[1], norm_mix[l])
        if l % 2 == 0:
            z = matmul(xn, w_in_even[e].astype(BF16))
            wai = jnp.concatenate([lru_wa[e], lru_wx[e]], axis=-1).astype(BF16)
            mix = (conv_a_w[e], conv_b_w[e], conv_b_b[e], wai, lru_ba[e], lru_bx[e], lru_lambda[e])
            y_pr, p_a, p_b, p_h = even_prompt(z, pb, pt, *mix)
            ka, kb = conv_a_w.shape[1], conv_b_w.shape[1]
            y_sm, s_a, s_b, s_h = even_sample(z, n_p, state_conv_a[e].reshape(sb, (ka - 1) * w),
                                              state_conv_b[e].reshape(sb, (kb - 1) * w), state_lru_h[e], *mix)
            new["pa"].append(p_a)
            new["pb"].append(p_b)
            new["ph"].append(p_h.reshape(pb, w))
            new["sa"].append(s_a.reshape(sb, ka - 1, w))
            new["sb"].append(s_b.reshape(sb, kb - 1, w))
            new["sh"].append(s_h)
            w_out = w_out_even[e].astype(BF16)
        else:
            n_main = 2 * n_h * dk + 2 * n_h * dv
            rank = w_in_odd.shape[2] - n_main
            w_in_t = jnp.swapaxes(w_in_odd[e], 0, 1).astype(BF16)
            z = matmul_nt(xn, w_in_t, n=n_main)
            w_gate_t = jnp.zeros((LANES, d), BF16).at[:rank].set(w_in_t[n_main:])
            zg = matmul_nt(xn, w_gate_t)
            wa2p = jnp.zeros((LANES, n_h * dk), BF16).at[:rank].set(gla_wa2[e].astype(BF16))
            y_pr, p_s = gla_prompt(z, zg, pb, pt, n_h, dk, dv, wa2p, gla_ba[e], gla_norm[e])
            y_sm, s_s = gla_sample(z, zg, n_p, state_gla[e], wa2p, gla_ba[e], gla_norm[e])
            new["ps"].append(p_s)
            new["ss"].append(s_s)
            w_out = w_out_odd[e].astype(BF16)
        x = matmul_add_rows(y_pr, w_out, x, 0)
        x = matmul_add_rows(y_sm, w_out, x, n_p)
        pending = moe_layer(x, norm_ffn[l], router_group_w[l], router_group_b[l], router_expert_w[l],
                            router_expert_b[l], moe_w_gate, moe_w_up, moe_w_down, l)

    y_p = combine_final(x, pending[0], pending[1], norm_final, 0, n_p).reshape(pb, pt, d)
    y_s = combine_final(x, pending[0], pending[1], norm_final, n_p, sb).reshape(sb, 1, d)
    stack = lambda k: jnp.stack(new[k])
    return (y_p, y_s, stack("pa"), stack("pb"), stack("ph"), stack("ps"),
            stack("sa"), stack("sb"), stack("sh"), stack("ss"))
```

```python
import functools

import jax
import jax.numpy as jnp
from jax import lax
from jax.experimental import pallas as pl
from jax.experimental.pallas import tpu as pltpu

F32 = jnp.float32
BF16 = jnp.bfloat16
I32 = jnp.int32

EPS = 1e-6
LRU_C = 8.0
GLA_GATE_NORM = 16.0
TOP_K = 2
LANES = 128
NEG = -1e30
VMEM_LIMIT = 56 * 1024 * 1024


def _cp(sem, vmem=VMEM_LIMIT):
    return pltpu.CompilerParams(dimension_semantics=sem, vmem_limit_bytes=vmem)


def _tile(n, cands):
    for c in cands:
        if n % c == 0:
            return c
    return n


def _sigmoid(x):
    return 0.5 * jnp.tanh(0.5 * x) + 0.5


def _softplus(x):
    return jnp.maximum(x, 0.0) + jnp.log(1.0 + jnp.exp(-jnp.abs(x)))


def _log_sigmoid(x):
    return -_softplus(-x)


def _one_minus_exp2(x):
    t = jnp.tanh(x)
    return (-2.0 * t) / (1.0 - t)


def _gelu_tanh(x):
    c = 0.7978845608028654
    return 0.5 * x * (1.0 + jnp.tanh(c * (x + 0.044715 * x * x * x)))


def _rms(x, g):
    return x * lax.rsqrt(jnp.mean(x * x, axis=-1, keepdims=True) + EPS) * g


def _combine(x_ref, o0_ref, o1_ref, w_ref):
    w = w_ref[...]
    return x_ref[...] + (o0_ref[...] * w[:, 0:1] + o1_ref[...] * w[:, 1:2])


def _combine_norm_kernel(x_ref, o0_ref, o1_ref, w_ref, g_ref, xnew_ref, xn_ref):
    x = _combine(x_ref, o0_ref, o1_ref, w_ref)
    xnew_ref[...] = x
    xn_ref[...] = _rms(x, g_ref[...]).astype(xn_ref.dtype)


def _combine_final_kernel(x_ref, o0_ref, o1_ref, w_ref, g_ref, y_ref):
    y_ref[...] = _rms(_combine(x_ref, o0_ref, o1_ref, w_ref), g_ref[...])


def _router_kernel(x_ref, g_ref, wr_ref, br_ref, xn_ref, eid_ref, wts_ref, *, n_groups, per_group):
    xn = _rms(x_ref[...], g_ref[...])
    xn_ref[...] = xn
    logits = jnp.dot(xn.astype(BF16), wr_ref[...], preferred_element_type=F32) + br_ref[...]
    tm = logits.shape[0]
    n_exp = n_groups * per_group
    lane = lax.broadcasted_iota(I32, (tm, LANES), 1)
    is_g = lane < n_groups
    lg = jnp.where(is_g, logits, NEG)
    mg = jnp.max(lg, axis=1, keepdims=True)
    gidx = jnp.min(jnp.where(is_g & (lg == mg), lane, LANES), axis=1, keepdims=True)
    pg = 1.0 / jnp.sum(jnp.where(is_g, jnp.exp(lg - mg), 0.0), axis=1, keepdims=True)
    lo = n_groups + gidx * per_group
    in_grp = (lane >= lo) & (lane < lo + per_group) & (lane < n_groups + n_exp)
    le = jnp.where(in_grp, logits, NEG)
    m1 = jnp.max(le, axis=1, keepdims=True)
    i1 = jnp.min(jnp.where(in_grp & (le == m1), lane, LANES), axis=1, keepdims=True)
    rest = in_grp & (lane != i1)
    le2 = jnp.where(rest, logits, NEG)
    m2 = jnp.max(le2, axis=1, keepdims=True)
    i2 = jnp.min(jnp.where(rest & (le2 == m2), lane, LANES), axis=1, keepdims=True)
    t = jnp.exp(m2 - m1)
    w1 = pg / (1.0 + t)
    w2 = pg * t / (1.0 + t)
    eid_ref[...] = jnp.where(lane == 0, i1 - n_groups, jnp.where(lane == 1, i2 - n_groups, 0))
    wts_ref[...] = jnp.where(lane == 0, w1, jnp.where(lane == 1, w2, 0.0))


def _row_spec(tm, d, off=0):
    return pl.BlockSpec((tm, d), lambda i: (i + off, 0))


def _const_spec(shape):
    return pl.BlockSpec(shape, lambda *_: tuple(0 for _ in shape))


def _stack_norm_kernel(xp_ref, xs_ref, g_ref, x_ref, xn_ref, *, nbp):
    def emit(src_ref):
        x = src_ref[...]
        x_ref[...] = x
        xn_ref[...] = _rms(x, g_ref[...]).astype(xn_ref.dtype)

    pl.when(pl.program_id(0) < nbp)(lambda: emit(xp_ref))
    pl.when(pl.program_id(0) >= nbp)(lambda: emit(xs_ref))


def stack_norm(xp, xs, g):
    (n_p, d), n_s = xp.shape, xs.shape[0]
    tm = next(c for c in (128, 64, 32, 16, 8) if n_p % c == 0 and n_s % c == 0)
    nbp, nbs = n_p // tm, n_s // tm
    kern = functools.partial(_stack_norm_kernel, nbp=nbp)
    return pl.pallas_call(
        kern, grid=(nbp + nbs,),
        in_specs=[pl.BlockSpec((tm, d), lambda i: (jnp.minimum(i, nbp - 1), 0)),
                  pl.BlockSpec((tm, d), lambda i: (jnp.maximum(i - nbp, 0), 0)),
                  _const_spec((1, d))],
        out_specs=[_row_spec(tm, d), _row_spec(tm, d)],
        out_shape=[jax.ShapeDtypeStruct((n_p + n_s, d), F32), jax.ShapeDtypeStruct((n_p + n_s, d), BF16)],
        compiler_params=_cp(("parallel",)), name="stack_norm",
    )(xp, xs, g.reshape(1, d))


def combine_norm(x, o, w, g):
    n, d = x.shape
    tm = _tile(n, (128, 64, 32, 16, 8))
    nb = n // tm
    return pl.pallas_call(
        _combine_norm_kernel, grid=(nb,),
        in_specs=[_row_spec(tm, d), _row_spec(tm, d), _row_spec(tm, d, nb), _row_spec(tm, LANES),
                  _const_spec((1, d))],
        out_specs=[_row_spec(tm, d), _row_spec(tm, d)],
        out_shape=[jax.ShapeDtypeStruct((n, d), F32), jax.ShapeDtypeStruct((n, d), BF16)],
        compiler_params=_cp(("parallel",)), name="combine_norm",
    )(x, o, o, w, g.reshape(1, d))


def combine_final(x, o, w, g, row0, rows):
    n, d = x.shape
    tm = next(c for c in (128, 64, 32, 16, 8) if rows % c == 0 and row0 % c == 0 and n % c == 0)
    off = row0 // tm
    return pl.pallas_call(
        _combine_final_kernel, grid=(rows // tm,),
        in_specs=[_row_spec(tm, d, off), _row_spec(tm, d, off), _row_spec(tm, d, off + n // tm),
                  _row_spec(tm, LANES, off), _const_spec((1, d))],
        out_specs=_row_spec(tm, d),
        out_shape=jax.ShapeDtypeStruct((rows, d), F32),
        compiler_params=_cp(("parallel",)), name="combine_final",
    )(x, o, o, w, g.reshape(1, d))


def router(x, g, wr, br, n_groups, per_group):
    n, d = x.shape
    tm = _tile(n, (320, 256, 128, 64, 32, 16, 8))
    kern = functools.partial(_router_kernel, n_groups=n_groups, per_group=per_group)
    return pl.pallas_call(
        kern, grid=(n // tm,),
        in_specs=[_row_spec(tm, d), _const_spec((1, d)), _const_spec((d, LANES)), _const_spec((1, LANES))],
        out_specs=[_row_spec(tm, d), _row_spec(tm, LANES), _row_spec(tm, LANES)],
        out_shape=[jax.ShapeDtypeStruct((n, d), F32), jax.ShapeDtypeStruct((n, LANES), I32),
                   jax.ShapeDtypeStruct((n, LANES), F32)],
        compiler_params=_cp(("parallel",)), name="router",
    )(x, g.reshape(1, d), wr, br)


def _mm_kernel(a_ref, b_ref, o_ref):
    o_ref[...] = jnp.dot(a_ref[...], b_ref[...], preferred_element_type=F32).astype(o_ref.dtype)


def _mm_res_kernel(a_ref, b_ref, r_ref, o_ref):
    o_ref[...] = r_ref[...] + jnp.dot(a_ref[...], b_ref[...], preferred_element_type=F32)


MM_TM = (1024, 832, 640, 512, 256, 128, 64, 32, 16)
MM_TN = (1024, 512, 256, 128)


def matmul(a, b, out_dtype=F32, n=None):
    m, k = a.shape
    n = b.shape[1] if n is None else n
    tm, tn = _tile(m, MM_TM), _tile(n, MM_TN)
    return pl.pallas_call(
        _mm_kernel, grid=(m // tm, n // tn),
        in_specs=[pl.BlockSpec((tm, k), lambda i, j: (i, 0)), pl.BlockSpec((k, tn), lambda i, j: (0, j))],
        out_specs=pl.BlockSpec((tm, tn), lambda i, j: (i, j)),
        out_shape=jax.ShapeDtypeStruct((m, n), out_dtype),
        compiler_params=_cp(("parallel", "parallel")), name="matmul",
    )(a, b)


def _mm_nt_kernel(a_ref, bt_ref, o_ref):
    o_ref[...] = lax.dot_general(a_ref[...], bt_ref[...], (((1,), (1,)), ((), ())),
                                 preferred_element_type=F32).astype(o_ref.dtype)


def matmul_nt(a, bt, out_dtype=F32, n=None):
    m, k = a.shape
    n = bt.shape[0] if n is None else n
    tm, tn = _tile(m, MM_TM), _tile(n, MM_TN)
    return pl.pallas_call(
        _mm_nt_kernel, grid=(m // tm, n // tn),
        in_specs=[pl.BlockSpec((tm, k), lambda i, j: (i, 0)), pl.BlockSpec((tn, k), lambda i, j: (j, 0))],
        out_specs=pl.BlockSpec((tm, tn), lambda i, j: (i, j)),
        out_shape=jax.ShapeDtypeStruct((m, n), out_dtype),
        compiler_params=_cp(("parallel", "parallel")), name="matmul_nt",
    )(a, bt)


def matmul_add_rows(a, b, x, row0):
    m, k = a.shape
    _, n = b.shape
    tm, tn = _tile(m, MM_TM), _tile(n, MM_TN)
    assert row0 % tm == 0
    off = row0 // tm
    xs = pl.BlockSpec((tm, tn), lambda i, j: (i + off, j))
    return pl.pallas_call(
        _mm_res_kernel, grid=(m // tm, n // tn),
        in_specs=[pl.BlockSpec((tm, k), lambda i, j: (i, 0)), pl.BlockSpec((k, tn), lambda i, j: (0, j)), xs],
        out_specs=xs,
        out_shape=jax.ShapeDtypeStruct(x.shape, F32),
        input_output_aliases={2: 0},
        compiler_params=_cp(("parallel", "parallel")), name="matmul_add_rows",
    )(a, b, x)


def _lru_gates(xc, wai_ref, ba_ref, bx_ref, lam_ref, a_out, b_out):
    n_heads, hd, _ = wai_ref.shape
    for h in range(n_heads):
        sl = slice(h * hd, (h + 1) * hd)
        xh = xc[:, sl]
        ri = jnp.dot(xh.astype(BF16), wai_ref[h], preferred_element_type=F32)
        r = _sigmoid(ri[:, :hd] + ba_ref[:, sl])
        i = _sigmoid(ri[:, hd:] + bx_ref[:, sl])
        log_a = (-LRU_C) * r * _softplus(-lam_ref[:, sl])
        a_out[:, sl] = jnp.exp(log_a)
        b_out[:, sl] = jnp.sqrt(_one_minus_exp2(log_a)) * (i * xh)


def _even_prompt_kernel(gb_ref, gc_ref, xa_ref, xr_ref, gr_ref, caw_ref, cbw_ref, cbb_ref, wai_ref,
                        ba_ref, bx_ref, lam_ref, y_ref, pa_ref, pb_ref, ph_ref,
                        ubuf, rbuf, hcar, abuf, bbuf, hbuf):
    t = pl.program_id(1)
    tt, w = gb_ref.shape
    ka, kb = caw_ref.shape[0], cbw_ref.shape[0]

    @pl.when(t == 0)
    def _():
        ubuf[0:8, :] = jnp.zeros((8, w), F32)
        rbuf[0:8, :] = jnp.zeros((8, w), F32)
        hcar[...] = jnp.zeros_like(hcar)

    u = gc_ref[...] * xa_ref[...]
    ubuf[8:8 + tt, :] = u
    v = caw_ref[ka - 1:ka, :] * u
    for k in range(1, ka):
        v = v + caw_ref[ka - 1 - k:ka - k, :] * ubuf[8 - k:8 - k + tt, :]
    y_ref[:, 0:w] = (gb_ref[...] * v).astype(y_ref.dtype)

    xr = xr_ref[...]
    rbuf[8:8 + tt, :] = xr
    xc = cbw_ref[kb - 1:kb, :] * xr + cbb_ref[...]
    for k in range(1, kb):
        xc = xc + cbw_ref[kb - 1 - k:kb - k, :] * rbuf[8 - k:8 - k + tt, :]
    _lru_gates(xc, wai_ref, ba_ref, bx_ref, lam_ref, abuf, bbuf)

    def step(i, h):
        h = abuf[pl.ds(i, 1), :] * h + bbuf[pl.ds(i, 1), :]
        hbuf[pl.ds(i, 1), :] = h
        return h

    h = lax.fori_loop(0, tt, step, hcar[0:1, :], unroll=8)
    hcar[0:1, :] = h
    y_ref[:, w:2 * w] = (_gelu_tanh(gr_ref[...]) * hbuf[...]).astype(y_ref.dtype)

    pa_ref[0] = ubuf[8 + tt - (ka - 1):8 + tt, :]
    pb_ref[0] = rbuf[8 + tt - (kb - 1):8 + tt, :]
    ph_ref[0] = h
    ubuf[0:8, :] = ubuf[tt:tt + 8, :]
    rbuf[0:8, :] = rbuf[tt:tt + 8, :]


def even_prompt(z, n_b, n_t, caw, cbw, cbb, wai, ba, bx, lam):
    n = n_b * n_t
    w = z.shape[1] // 5
    tt = _tile(n_t, (128, 64, 32, 16, 8))
    nt = n_t // tt
    ka, kb = caw.shape[0], cbw.shape[0]

    def zs(c):
        return pl.BlockSpec((tt, w), lambda b, t: (b * nt + t, c))

    return pl.pallas_call(
        _even_prompt_kernel, grid=(n_b, nt),
        in_specs=[zs(0), zs(1), zs(2), zs(3), zs(4), _const_spec((ka, w)), _const_spec((kb, w)),
                  _const_spec((1, w)), _const_spec(wai.shape), _const_spec((1, w)), _const_spec((1, w)),
                  _const_spec((1, w))],
        out_specs=[pl.BlockSpec((tt, 2 * w), lambda b, t: (b * nt + t, 0)),
                   pl.BlockSpec((1, ka - 1, w), lambda b, t: (b, 0, 0)),
                   pl.BlockSpec((1, kb - 1, w), lambda b, t: (b, 0, 0)),
                   pl.BlockSpec((1, 1, w), lambda b, t: (b, 0, 0))],
        out_shape=[jax.ShapeDtypeStruct((n, 2 * w), BF16),
                   jax.ShapeDtypeStruct((n_b, ka - 1, w), F32),
                   jax.ShapeDtypeStruct((n_b, kb - 1, w), F32),
                   jax.ShapeDtypeStruct((n_b, 1, w), F32)],
        scratch_shapes=[pltpu.VMEM((8 + tt, w), F32), pltpu.VMEM((8 + tt, w), F32), pltpu.VMEM((8, w), F32),
                        pltpu.VMEM((tt, w), F32), pltpu.VMEM((tt, w), F32), pltpu.VMEM((tt, w), F32)],
        compiler_params=_cp(("parallel", "arbitrary")), name="even_prompt",
    )(z, z, z, z, z, caw, cbw, cbb.reshape(1, w), wai, ba.reshape(1, w), bx.reshape(1, w), lam.reshape(1, w))


def _even_sample_kernel(gb_ref, gc_ref, xa_ref, xr_ref, gr_ref, sa_ref, sb_ref, sh_ref, caw_ref, cbw_ref,
                        cbb_ref, wai_ref, ba_ref, bx_ref, lam_ref, y_ref, na_ref, nb_ref, nh_ref,
                        abuf, bbuf):
    w = gb_ref.shape[1]
    ka, kb = caw_ref.shape[0], cbw_ref.shape[0]
    u = gc_ref[...] * xa_ref[...]
    v = caw_ref[ka - 1:ka, :] * u
    for k in range(1, ka):
        v = v + caw_ref[ka - 1 - k:ka - k, :] * sa_ref[:, (ka - 1 - k) * w:(ka - k) * w]
    y_ref[:, 0:w] = (gb_ref[...] * v).astype(y_ref.dtype)
    for k in range(ka - 2):
        na_ref[:, k * w:(k + 1) * w] = sa_ref[:, (k + 1) * w:(k + 2) * w]
    na_ref[:, (ka - 2) * w:(ka - 1) * w] = u

    xr = xr_ref[...]
    xc = cbw_ref[kb - 1:kb, :] * xr + cbb_ref[...]
    for k in range(1, kb):
        xc = xc + cbw_ref[kb - 1 - k:kb - k, :] * sb_ref[:, (kb - 1 - k) * w:(kb - k) * w]
    for k in range(kb - 2):
        nb_ref[:, k * w:(k + 1) * w] = sb_ref[:, (k + 1) * w:(k + 2) * w]
    nb_ref[:, (kb - 2) * w:(kb - 1) * w] = xr

    _lru_gates(xc, wai_ref, ba_ref, bx_ref, lam_ref, abuf, bbuf)
    h = abuf[...] * sh_ref[...] + bbuf[...]
    nh_ref[...] = h
    y_ref[:, w:2 * w] = (_gelu_tanh(gr_ref[...]) * h).astype(y_ref.dtype)


def even_sample(z, row0, sa, sb, sh, caw, cbw, cbb, wai, ba, bx, lam):
    s, w = sh.shape
    ka, kb = caw.shape[0], cbw.shape[0]
    assert row0 % s == 0
    rb = row0 // s

    def zs(c):
        return pl.BlockSpec((s, w), lambda i: (rb, c))

    full = lambda shape: pl.BlockSpec(shape, lambda i: tuple(0 for _ in shape))
    return pl.pallas_call(
        _even_sample_kernel, grid=(1,),
        in_specs=[zs(0), zs(1), zs(2), zs(3), zs(4), full(sa.shape), full(sb.shape), full(sh.shape),
                  full((ka, w)), full((kb, w)), full((1, w)), full(wai.shape), full((1, w)), full((1, w)),
                  full((1, w))],
        out_specs=[full((s, 2 * w)), full(sa.shape), full(sb.shape), full(sh.shape)],
        out_shape=[jax.ShapeDtypeStruct((s, 2 * w), BF16), jax.ShapeDtypeStruct(sa.shape, F32),
                   jax.ShapeDtypeStruct(sb.shape, F32), jax.ShapeDtypeStruct(sh.shape, F32)],
        scratch_shapes=[pltpu.VMEM((s, w), F32), pltpu.VMEM((s, w), F32)],
        compiler_params=_cp(("arbitrary",)), name="even_sample",
    )(z, z, z, z, z, sa, sb, sh, caw, cbw, cbb.reshape(1, w), wai, ba.reshape(1, w), bx.reshape(1, w),
      lam.reshape(1, w))


def _gla_log_decay(gr_ref, wa2_ref, ba_ref):
    pre = jnp.dot(gr_ref[...].astype(BF16), wa2_ref[...], preferred_element_type=F32) + ba_ref[...]
    return _log_sigmoid(pre) * (1.0 / GLA_GATE_NORM)


def _head_norm_gate(o, gn_ref, g):
    o = o * lax.rsqrt(jnp.mean(o * o, axis=-1, keepdims=True) + EPS) * gn_ref[...]
    return o * (g * _sigmoid(g))


def _cumsum_rows(tri16, x):
    hi = x.astype(BF16)
    r1 = x - hi.astype(F32)
    mid = r1.astype(BF16)
    lo = (r1 - mid.astype(F32)).astype(BF16)
    dot = lambda t: jnp.dot(tri16, t, preferred_element_type=F32)
    return dot(hi) + dot(mid) + dot(lo)


GLA_SUB = 16
GLA_SAFE_LOG = 60.0


def _gla_rows(r0, cs, q_ref, k_ref, v_ref, g_ref, gn_ref, o_ref, gabuf, st):
    n_h, dv, dk = st.shape
    rows = pl.ds(r0, cs)
    row = lax.broadcasted_iota(I32, (cs, cs), 0)
    col = lax.broadcasted_iota(I32, (cs, cs), 1)
    causal = row >= col
    b = _cumsum_rows(causal.astype(BF16), gabuf[rows, :])
    b_last = b[cs - 1:cs, :]
    k = k_ref[rows, :]
    qe_all = (q_ref[rows, :] * (dk ** -0.5) * jnp.exp(b)).astype(BF16)
    ke_all = (k * jnp.exp(-b)).astype(BF16)
    kd_all = (k * jnp.exp(b_last - b)).astype(BF16)
    decay = jnp.exp(b_last)
    nt = (((1,), (1,)), ((), ()))
    tn = (((0,), (0,)), ((), ()))
    for h in range(n_h):
        sk, sv = slice(h * dk, (h + 1) * dk), slice(h * dv, (h + 1) * dv)
        qe, ke, kd = qe_all[:, sk], ke_all[:, sk], kd_all[:, sk]
        v16 = v_ref[rows, sv].astype(BF16)
        att = lax.dot_general(qe, ke, nt, preferred_element_type=F32)
        att = jnp.where(causal, att, 0.0).astype(BF16)
        s_old = st[h]
        o = lax.dot_general(qe, s_old.astype(BF16), nt, preferred_element_type=F32)
        o = o + jnp.dot(att, v16, preferred_element_type=F32)
        st[h] = s_old * decay[:, sk] + lax.dot_general(v16, kd, tn, preferred_element_type=F32)
        o_ref[rows, sv] = _head_norm_gate(o, gn_ref, g_ref[rows, sv]).astype(o_ref.dtype)


def _gla_prompt_kernel(q_ref, k_ref, v_ref, g_ref, gr_ref, wa2_ref, ba_ref, gn_ref, o_ref, s_ref, gabuf, st):
    c = pl.program_id(1)
    cs = q_ref.shape[0]

    @pl.when(c == 0)
    def _():
        st[...] = jnp.zeros_like(st)

    ga = _gla_log_decay(gr_ref, wa2_ref, ba_ref)
    gabuf[...] = ga
    one_piece = jnp.min(jnp.sum(ga, axis=0, keepdims=True)) > -GLA_SAFE_LOG
    args = (q_ref, k_ref, v_ref, g_ref, gn_ref, o_ref, gabuf, st)

    @pl.when(one_piece)
    def _():
        _gla_rows(0, cs, *args)

    @pl.when(jnp.logical_not(one_piece))
    def _():
        def sub(j, carry):
            _gla_rows(pl.multiple_of(j * GLA_SUB, GLA_SUB), GLA_SUB, *args)
            return carry
        lax.fori_loop(0, cs // GLA_SUB, sub, 0)

    @pl.when(c == pl.num_programs(1) - 1)
    def _():
        for h in range(st.shape[0]):
            s_ref[0, h] = st[h].T


def gla_prompt(z, zg, n_b, n_t, n_h, dk, dv, wa2p, ba, gnorm):
    n = n_b * n_t
    cs = _tile(n_t, (256, 128, 64, 32, 16))
    nc = n_t // cs
    dkt, dvt = n_h * dk, n_h * dv
    assert (2 * dkt) % dvt == 0 and cs % GLA_SUB == 0
    vb0 = (2 * dkt) // dvt
    row = lambda b, c: b * nc + c
    return pl.pallas_call(
        _gla_prompt_kernel, grid=(n_b, nc),
        in_specs=[pl.BlockSpec((cs, dkt), lambda b, c: (row(b, c), 0)),
                  pl.BlockSpec((cs, dkt), lambda b, c: (row(b, c), 1)),
                  pl.BlockSpec((cs, dvt), lambda b, c: (row(b, c), vb0)),
                  pl.BlockSpec((cs, dvt), lambda b, c: (row(b, c), vb0 + 1)),
                  pl.BlockSpec((cs, LANES), lambda b, c: (row(b, c), 0)),
                  _const_spec((LANES, dkt)), _const_spec((1, dkt)), _const_spec((1, dv))],
        out_specs=[pl.BlockSpec((cs, dvt), lambda b, c: (row(b, c), 0)),
                   pl.BlockSpec((1, n_h, dk, dv), lambda b, c: (b, 0, 0, 0))],
        out_shape=[jax.ShapeDtypeStruct((n, dvt), BF16),
                   jax.ShapeDtypeStruct((n_b, n_h, dk, dv), F32)],
        scratch_shapes=[pltpu.VMEM((cs, dkt), F32), pltpu.VMEM((n_h, dv, dk), F32)],
        compiler_params=_cp(("parallel", "arbitrary")), name="gla_prompt",
    )(z, z, z, z, zg, wa2p, ba.reshape(1, -1), gnorm.reshape(1, dv))


def _gla_sample_kernel(q_ref, k_ref, v_ref, g_ref, gr_ref, wa2_ref, ba_ref, gn_ref, s0_ref,
                       o_ref, s_ref, ta, tk, tq):
    i = pl.program_id(1)
    nblk, dk, bb = ta.shape

    @pl.when(i == 0)
    def _():
        alpha_t = jnp.exp(_gla_log_decay(gr_ref, wa2_ref, ba_ref)).T
        k_t = k_ref[...].T
        q_t = (q_ref[...] * (dk ** -0.5)).T
        for j in range(nblk):
            ta[j] = alpha_t[:, j * bb:(j + 1) * bb]
            tk[j] = k_t[:, j * bb:(j + 1) * bb]
            tq[j] = q_t[:, j * bb:(j + 1) * bb]

    a3, k3, q3 = ta[i], tk[i], tq[i]
    v = v_ref[...]
    rows = []
    for j in range(bb):
        s_new = s0_ref[j, 0] * a3[:, j:j + 1] + k3[:, j:j + 1] * v[j:j + 1, :]
        s_ref[j, 0] = s_new
        rows.append(jnp.sum(q3[:, j:j + 1] * s_new, axis=0, keepdims=True))
    o = jnp.concatenate(rows, axis=0)
    o_ref[...] = _head_norm_gate(o, gn_ref, g_ref[...]).astype(o_ref.dtype)


def gla_sample(z, zg, row0, s0, wa2p, ba, gnorm):
    s, n_h, dk, dv = s0.shape
    bb = _tile(s, (16, 8))
    nblk = s // bb
    assert row0 % s == 0 and (2 * n_h * dk) % dv == 0
    rs, rb = row0 // s, row0 // bb
    kb0 = n_h
    vb0 = (2 * n_h * dk) // dv
    gb0 = vb0 + n_h
    return pl.pallas_call(
        _gla_sample_kernel, grid=(n_h, nblk),
        in_specs=[pl.BlockSpec((s, dk), lambda h, i: (rs, h)),
                  pl.BlockSpec((s, dk), lambda h, i: (rs, kb0 + h)),
                  pl.BlockSpec((bb, dv), lambda h, i: (rb + i, vb0 + h)),
                  pl.BlockSpec((bb, dv), lambda h, i: (rb + i, gb0 + h)),
                  pl.BlockSpec((s, LANES), lambda h, i: (rs, 0)),
                  pl.BlockSpec((LANES, dk), lambda h, i: (0, h)),
                  pl.BlockSpec((1, dk), lambda h, i: (0, h)),
                  pl.BlockSpec((1, dv), lambda h, i: (0, 0)),
                  pl.BlockSpec((bb, 1, dk, dv), lambda h, i: (i, h, 0, 0))],
        out_specs=[pl.BlockSpec((bb, dv), lambda h, i: (i, h)),
                   pl.BlockSpec((bb, 1, dk, dv), lambda h, i: (i, h, 0, 0))],
        out_shape=[jax.ShapeDtypeStruct((s, n_h * dv), BF16), jax.ShapeDtypeStruct(s0.shape, F32)],
        scratch_shapes=[pltpu.VMEM((nblk, dk, bb), F32)] * 3,
        compiler_params=_cp(("arbitrary", "arbitrary")), name="gla_sample",
    )(z, z, z, z, zg, wa2p, ba.reshape(1, -1), gnorm.reshape(1, dv), s0)


MOE_SHORT_TILE = 128
MOE_AHEAD = 4
MOE_ROW_UNROLL = 8
MOE_WEIGHT_DMA_PRIORITY = 1


def _moe_kernel(ie_ref, istart_ref, icnt_ref, tok_ref, dst_ref, xn_hbm, wg_hbm, wu_hbm, wd_hbm, o_hbm,
                xstage, obuf, xb16, hbuf, ring_up, ring_dn, wg16, wu16, wd16,
                sem_x, sem_o, sem_up, sem_dn, *, layer, tiles):
    it = pl.program_id(0)
    last = pl.num_programs(0) - 1
    e, start, cnt = ie_ref[it], istart_ref[it], icnt_ref[it]
    nxt = jnp.minimum(it + 1, last)
    has_next = (it < last) & (icnt_ref[nxt] > 0)
    e_next, start_next, cnt_next = ie_ref[nxt], istart_ref[nxt], icnt_ref[nxt]
    prv = jnp.maximum(it - 1, 0)
    start_prev, cnt_prev = istart_ref[prv], jnp.where(it > 0, icnt_ref[prv], 0)
    fc, dc = wg16.shape[1], wd16.shape[1]
    nf, nd = hbuf.shape[1] // fc, obuf.shape[1] // dc
    n_chunks = 2 * nf + nd
    up_slots, dn_slots = ring_up.shape[0], ring_dn.shape[0]
    ahead = dn_slots
    big, tails = tiles[0], tiles[1:]

    def chunk_copy(k, ex):
        if k < 2 * nf:
            w_hbm = wg_hbm if k % 2 == 0 else wu_hbm
            slot = k % up_slots
            return pltpu.make_async_copy(w_hbm.at[layer, ex, :, pl.ds((k // 2) * fc, fc)], ring_up.at[slot],
                                         sem_up.at[slot])
        c = k - 2 * nf
        slot = c % dn_slots
        return pltpu.make_async_copy(wd_hbm.at[layer, ex, :, pl.ds(c * dc, dc)], ring_dn.at[slot],
                                     sem_dn.at[slot])

    def prefetch(ks):
        for k in ks:
            if k < n_chunks:
                chunk_copy(k, e).start(priority=MOE_WEIGHT_DMA_PRIORITY)
            else:
                @pl.when(has_next)
                def _(k=k):
                    chunk_copy(k - n_chunks, e_next).start(priority=MOE_WEIGHT_DMA_PRIORITY)

    def gather_copy(r, base):
        return pltpu.make_async_copy(xn_hbm.at[pl.ds(tok_ref[base + r], 1), :], xstage.at[pl.ds(r, 1), :], sem_x)

    def scatter_copy(r, base):
        return pltpu.make_async_copy(obuf.at[pl.ds(r, 1), :], o_hbm.at[pl.ds(dst_ref[base + r], 1), :], sem_o)

    def for_rows(n, fn):
        n_grp = lax.shift_right_logical(n, MOE_ROW_UNROLL.bit_length() - 1)

        def group(g, c):
            for u in range(MOE_ROW_UNROLL):
                fn(pl.multiple_of(g * MOE_ROW_UNROLL, MOE_ROW_UNROLL) + u)
            return c
        lax.fori_loop(0, n_grp, group, 0)

        def single(r, c):
            fn(r)
            return c
        lax.fori_loop(n_grp * MOE_ROW_UNROLL, n, single, 0)

    def for_tiles(fn):
        rem = cnt % big
        n_big = cnt // big + (rem > (tails[0] if tails else 0)).astype(I32)

        def body(j, c):
            fn(pl.multiple_of(j * big, big), big)
            return c
        lax.fori_loop(0, n_big, body, 0)

        for t, t_below in zip(tails, tails[1:] + (0,)):
            @pl.when((rem > t_below) & (rem <= t))
            def _(t=t):
                fn(pl.multiple_of(n_big * big, big), t)

    @pl.when(it == 0)
    def _():
        xstage[...] = jnp.zeros_like(xstage)

    @pl.when((it == 0) & (cnt > 0))
    def _():
        prefetch(range(ahead))
        for_rows(cnt, lambda r: gather_copy(r, start).start())

    @pl.when(cnt > 0)
    def _():
        for_rows(cnt, lambda r: gather_copy(r, start).wait())

        def cast(r0, t):
            xb16[pl.ds(r0, t), :] = xstage[pl.ds(r0, t), :].astype(BF16)
        for_tiles(cast)

        @pl.when(has_next)
        def _():
            for_rows(cnt_next, lambda r: gather_copy(r, start_next).start())

        for c in range(nf):
            chunk_copy(2 * c, e).wait()
            wg16[...] = ring_up[(2 * c) % up_slots].astype(BF16)
            chunk_copy(2 * c + 1, e).wait()
            wu16[...] = ring_up[(2 * c + 1) % up_slots].astype(BF16)
            prefetch((2 * c + ahead, 2 * c + ahead + 1))

            def up(r0, t, c=c):
                x = xb16[pl.ds(r0, t), :]
                g = jnp.dot(x, wg16[...], preferred_element_type=F32)
                u = jnp.dot(x, wu16[...], preferred_element_type=F32)
                hbuf[pl.ds(r0, t), c * fc:(c + 1) * fc] = (g * _sigmoid(g) * u).astype(BF16)
            for_tiles(up)

        for_rows(cnt_prev, lambda r: scatter_copy(r, start_prev).wait())

        for c in range(nd):
            k = 2 * nf + c
            chunk_copy(k, e).wait()
            wd16[...] = ring_dn[c % dn_slots].astype(BF16)
            prefetch((k + ahead,))

            def down(r0, t, c=c):
                obuf[pl.ds(r0, t), c * dc:(c + 1) * dc] = jnp.dot(hbuf[pl.ds(r0, t), :], wd16[...],
                                                                 preferred_element_type=F32)
            for_tiles(down)

        for_rows(cnt, lambda r: scatter_copy(r, start).start())

        @pl.when(jnp.logical_not(has_next))
        def _():
            for_rows(cnt, lambda r: scatter_copy(r, start).wait())


def moe_route_tables(eid, n_exp, rows):
    n = eid.shape[0]
    flat = eid.reshape(-1)
    na = flat.shape[0]
    order = jnp.argsort(flat).astype(I32)
    counts = jnp.bincount(flat, length=n_exp).astype(I32)
    gstart = jnp.cumsum(counts) - counts
    n_it = (counts + rows - 1) // rows
    it_cum = jnp.cumsum(n_it)
    it_base = it_cum - n_it
    total = it_cum[-1]
    n_items = n_exp + na // rows
    ii = jnp.arange(n_items, dtype=I32)
    e_i = jnp.minimum(jnp.searchsorted(it_cum, ii, side="right").astype(I32), n_exp - 1)
    valid = ii < total
    e_i = jnp.where(valid, e_i, e_i[jnp.maximum(total - 1, 0)])
    j = ii - it_base[e_i]
    start = jnp.where(valid, gstart[e_i] + j * rows, 0).astype(I32)
    cnt = jnp.where(valid, jnp.minimum(rows, counts[e_i] - j * rows), 0).astype(I32)
    tok = order // TOP_K
    dst = (order % TOP_K) * n + tok
    return e_i, start, cnt, tok, dst


def _moe_tiles(n_assign, n_exp):
    mean = n_assign / n_exp
    big = max(MOE_SHORT_TILE, -(-int(mean + 2.0 * mean ** 0.5) // 32) * 32)
    return (big, MOE_SHORT_TILE) if big > MOE_SHORT_TILE else (big,)


def moe_ffn(xn, eid, wg, wu, wd, layer):
    n, d = xn.shape
    _, n_exp, _, f = wg.shape
    tiles = _moe_tiles(TOP_K * n, n_exp)
    rows = tiles[0]
    fc = _tile(f, (256, 128))
    dc = _tile(d, (1024, 512, 256, 128))
    nf, nd = f // fc, d // dc
    ahead = min(MOE_AHEAD, 2 * nf + nd, nd + 1)
    up_slots, dn_slots = ahead + 1, ahead
    e_i, start, cnt, tok, dst = moe_route_tables(eid, n_exp, rows)
    n_items = e_i.shape[0]
    kern = functools.partial(_moe_kernel, layer=layer, tiles=tiles)
    any_spec = pl.BlockSpec(memory_space=pl.ANY)
    grid_spec = pltpu.PrefetchScalarGridSpec(
        num_scalar_prefetch=5, grid=(n_items,),
        in_specs=[any_spec, any_spec, any_spec, any_spec],
        out_specs=any_spec,
        scratch_shapes=[pltpu.VMEM((rows, d), F32), pltpu.VMEM((rows, d), F32), pltpu.VMEM((rows, d), BF16),
                        pltpu.VMEM((rows, f), BF16),
                        pltpu.VMEM((up_slots, d, fc), F32), pltpu.VMEM((dn_slots, f, dc), F32),
                        pltpu.VMEM((d, fc), BF16), pltpu.VMEM((d, fc), BF16), pltpu.VMEM((f, dc), BF16),
                        pltpu.SemaphoreType.DMA(()), pltpu.SemaphoreType.DMA(()),
                        pltpu.SemaphoreType.DMA((up_slots,)), pltpu.SemaphoreType.DMA((dn_slots,))])
    return pl.pallas_call(
        kern, grid_spec=grid_spec,
        out_shape=jax.ShapeDtypeStruct((TOP_K * n, d), F32),
        compiler_params=_cp(("arbitrary",)), name="moe_ffn",
    )(e_i, start, cnt, tok, dst, xn, wg, wu, wd)


def moe_layer(x, g, wrg, brg, wre, bre, wg, wu, wd, layer):
    n_groups = wrg.shape[1]
    n_exp = wre.shape[1]
    d = x.shape[1]
    wr = jnp.zeros((d, LANES), BF16).at[:, :n_groups].set(wrg.astype(BF16))
    wr = wr.at[:, n_groups:n_groups + n_exp].set(wre.astype(BF16))
    br = jnp.zeros((1, LANES), F32).at[0, :n_groups].set(brg).at[0, n_groups:n_groups + n_exp].set(bre)
    xn, eid, wts = router(x, g, wr, br, n_groups, n_exp // n_groups)
    o = moe_ffn(xn, eid[:, :TOP_K], wg, wu, wd, layer)
    return o, wts


def kernel(x_prompt, x_sample, state_conv_a, state_conv_b, state_lru_h, state_gla, norm_mix, norm_ffn,
           norm_final, w_in_even, conv_a_w, conv_b_w, conv_b_b, lru_wa, lru_ba, lru_wx, lru_bx, lru_lambda,
           w_out_even, w_in_odd, gla_wa2, gla_ba, gla_norm, w_out_odd, router_group_w, router_group_b,
           router_expert_w, router_expert_b, moe_w_gate, moe_w_up, moe_w_down):
    pb, pt, d = x_prompt.shape
    sb, st_, _ = x_sample.shape
    assert st_ == 1
    n_p = pb * pt
    n = n_p + sb
    depth = norm_mix.shape[0]
    w = state_lru_h.shape[-1]
    n_h, dk, dv = state_gla.shape[2:]

    new = {k: [] for k in ("pa", "pb", "ph", "ps", "sa", "sb", "sh", "ss")}
    pending = None

    for l in range(depth):
        e = l // 2
        if pending is None:
            x, xn = stack_norm(x_prompt.reshape(n_p, d), x_sample.reshape(sb, d), norm_mix[l])
        else:
            x, xn = combine_norm(x, pending[0], pending[1], norm_mix[l])
        if l % 2 == 0:
            z = matmul(xn, w_in_even[e].astype(BF16))
            wai = jnp.concatenate([lru_wa[e], lru_wx[e]], axis=-1).astype(BF16)
            mix = (conv_a_w[e], conv_b_w[e], conv_b_b[e], wai, lru_ba[e], lru_bx[e], lru_lambda[e])
            y_pr, p_a, p_b, p_h = even_prompt(z, pb, pt, *mix)
            ka, kb = conv_a_w.shape[1], conv_b_w.shape[1]
            y_sm, s_a, s_b, s_h = even_sample(z, n_p, state_conv_a[e].reshape(sb, (ka - 1) * w),
                                              state_conv_b[e].reshape(sb, (kb - 1) * w), state_lru_h[e], *mix)
            new["pa"].append(p_a)
            new["pb"].append(p_b)
            new["ph"].append(p_h.reshape(pb, w))
            new["sa"].append(s_a.reshape(sb, ka - 1, w))
            new["sb"].append(s_b.reshape(sb, kb - 1, w))
            new["sh"].append(s_h)
            w_out = w_out_even[e].astype(BF16)
        else:
            n_main = 2 * n_h * dk + 2 * n_h * dv
            rank = w_in_odd.shape[2] - n_main
            w_in_t = jnp.swapaxes(w_in_odd[e], 0, 1).astype(BF16)
            z = matmul_nt(xn, w_in_t, n=n_main)
            w_gate_t = jnp.zeros((LANES, d), BF16).at[:rank].set(w_in_t[n_main:])
            zg = matmul_nt(xn, w_gate_t)
            wa2p = jnp.zeros((LANES, n_h * dk), BF16).at[:rank].set(gla_wa2[e].astype(BF16))
            y_pr, p_s = gla_prompt(z, zg, pb, pt, n_h, dk, dv, wa2p, gla_ba[e], gla_norm[e])
            y_sm, s_s = gla_sample(z, zg, n_p, state_gla[e], wa2p, gla_ba[e], gla_norm[e])
            new["ps"].append(p_s)
            new["ss"].append(s_s)
            w_out = w_out_odd[e].astype(BF16)
        x = matmul_add_rows(y_pr, w_out, x, 0)
        x = matmul_add_rows(y_sm, w_out, x, n_p)
        pending = moe_layer(x, norm_ffn[l], router_group_w[l], router_group_b[l], router_expert_w[l],
                            router_expert_b[l], moe_w_gate, moe_w_up, moe_w_down, l)

    y_p = combine_final(x, pending[0], pending[1], norm_final, 0, n_p).reshape(pb, pt, d)
    y_s = combine_final(x, pending[0], pending[1], norm_final, n_p, sb).reshape(sb, 1, d)
    stack = lambda k: jnp.stack(new[k])
    return (y_p, y_s, stack("pa"), stack("pb"), stack("ph"), stack("ps"),
            stack("sa"), stack("sb"), stack("sh"), stack("ss"))
```

```python
import functools

import jax
import jax.numpy as jnp
from jax import lax
from jax.experimental import pallas as pl
from jax.experimental.pallas import tpu as pltpu

F32 = jnp.float32
BF16 = jnp.bfloat16
I32 = jnp.int32

EPS = 1e-6
LRU_C = 8.0
GLA_GATE_NORM = 16.0
TOP_K = 2
LANES = 128
NEG = -1e30
VMEM_LIMIT = 56 * 1024 * 1024


def _cp(sem, vmem=VMEM_LIMIT):
    return pltpu.CompilerParams(dimension_semantics=sem, vmem_limit_bytes=vmem)


def _tile(n, cands):
    for c in cands:
        if n % c == 0:
            return c
    return n


def _sigmoid(x):
    return 0.5 * jnp.tanh(0.5 * x) + 0.5


def _softplus(x):
    return jnp.maximum(x, 0.0) + jnp.log(1.0 + jnp.exp(-jnp.abs(x)))


def _log_sigmoid(x):
    return -_softplus(-x)


def _one_minus_exp2(x):
    t = jnp.tanh(x)
    return (-2.0 * t) / (1.0 - t)


def _gelu_tanh(x):
    c = 0.7978845608028654
    return 0.5 * x * (1.0 + jnp.tanh(c * (x + 0.044715 * x * x * x)))


def _rms(x, g):
    return x * lax.rsqrt(jnp.mean(x * x, axis=-1, keepdims=True) + EPS) * g


def _combine(x_ref, o0_ref, o1_ref, w_ref):
    w = w_ref[...]
    return x_ref[...] + (o0_ref[...] * w[:, 0:1] + o1_ref[...] * w[:, 1:2])


def _combine_norm_kernel(x_ref, o0_ref, o1_ref, w_ref, g_ref, xnew_ref, xn_ref):
    x = _combine(x_ref, o0_ref, o1_ref, w_ref)
    xnew_ref[...] = x
    xn_ref[...] = _rms(x, g_ref[...]).astype(xn_ref.dtype)


def _combine_final_kernel(x_ref, o0_ref, o1_ref, w_ref, g_ref, y_ref):
    y_ref[...] = _rms(_combine(x_ref, o0_ref, o1_ref, w_ref), g_ref[...])


def _router_kernel(x_ref, g_ref, wr_ref, br_ref, eid_ref, wts_ref, *, n_groups, per_group):
    xn = _rms(x_ref[...], g_ref[...])
    logits = jnp.dot(xn.astype(BF16), wr_ref[...], preferred_element_type=F32) + br_ref[...]
    tm = logits.shape[0]
    n_exp = n_groups * per_group
    lane = lax.broadcasted_iota(I32, (tm, LANES), 1)
    is_g = lane < n_groups
    lg = jnp.where(is_g, logits, NEG)
    mg = jnp.max(lg, axis=1, keepdims=True)
    gidx = jnp.min(jnp.where(is_g & (lg == mg), lane, LANES), axis=1, keepdims=True)
    pg = 1.0 / jnp.sum(jnp.where(is_g, jnp.exp(lg - mg), 0.0), axis=1, keepdims=True)
    lo = n_groups + gidx * per_group
    in_grp = (lane >= lo) & (lane < lo + per_group) & (lane < n_groups + n_exp)
    le = jnp.where(in_grp, logits, NEG)
    m1 = jnp.max(le, axis=1, keepdims=True)
    i1 = jnp.min(jnp.where(in_grp & (le == m1), lane, LANES), axis=1, keepdims=True)
    rest = in_grp & (lane != i1)
    le2 = jnp.where(rest, logits, NEG)
    m2 = jnp.max(le2, axis=1, keepdims=True)
    i2 = jnp.min(jnp.where(rest & (le2 == m2), lane, LANES), axis=1, keepdims=True)
    t = jnp.exp(m2 - m1)
    w1 = pg / (1.0 + t)
    w2 = pg * t / (1.0 + t)
    eid_ref[...] = jnp.where(lane == 0, i1 - n_groups, jnp.where(lane == 1, i2 - n_groups, 0))
    wts_ref[...] = jnp.where(lane == 0, w1, jnp.where(lane == 1, w2, 0.0))


def _row_spec(tm, d, off=0):
    return pl.BlockSpec((tm, d), lambda i: (i + off, 0))


def _const_spec(shape):
    return pl.BlockSpec(shape, lambda *_: tuple(0 for _ in shape))


def _stack_norm_kernel(xp_ref, xs_ref, g_ref, x_ref, xn_ref, *, nbp):
    def emit(src_ref):
        x = src_ref[...]
        x_ref[...] = x
        xn_ref[...] = _rms(x, g_ref[...]).astype(xn_ref.dtype)

    pl.when(pl.program_id(0) < nbp)(lambda: emit(xp_ref))
    pl.when(pl.program_id(0) >= nbp)(lambda: emit(xs_ref))


def stack_norm(xp, xs, g):
    (n_p, d), n_s = xp.shape, xs.shape[0]
    tm = next(c for c in (128, 64, 32, 16, 8) if n_p % c == 0 and n_s % c == 0)
    nbp, nbs = n_p // tm, n_s // tm
    kern = functools.partial(_stack_norm_kernel, nbp=nbp)
    return pl.pallas_call(
        kern, grid=(nbp + nbs,),
        in_specs=[pl.BlockSpec((tm, d), lambda i: (jnp.minimum(i, nbp - 1), 0)),
                  pl.BlockSpec((tm, d), lambda i: (jnp.maximum(i - nbp, 0), 0)),
                  _const_spec((1, d))],
        out_specs=[_row_spec(tm, d), _row_spec(tm, d)],
        out_shape=[jax.ShapeDtypeStruct((n_p + n_s, d), F32), jax.ShapeDtypeStruct((n_p + n_s, d), BF16)],
        compiler_params=_cp(("parallel",)), name="stack_norm",
    )(xp, xs, g.reshape(1, d))


def combine_norm(x, o, w, g):
    n, d = x.shape
    tm = _tile(n, (128, 64, 32, 16, 8))
    nb = n // tm
    return pl.pallas_call(
        _combine_norm_kernel, grid=(nb,),
        in_specs=[_row_spec(tm, d), _row_spec(tm, d), _row_spec(tm, d, nb), _row_spec(tm, LANES),
                  _const_spec((1, d))],
        out_specs=[_row_spec(tm, d), _row_spec(tm, d)],
        out_shape=[jax.ShapeDtypeStruct((n, d), F32), jax.ShapeDtypeStruct((n, d), BF16)],
        compiler_params=_cp(("parallel",)), name="combine_norm",
    )(x, o, o, w, g.reshape(1, d))


def combine_final(x, o, w, g, row0, rows):
    n, d = x.shape
    tm = next(c for c in (128, 64, 32, 16, 8) if rows % c == 0 and row0 % c == 0 and n % c == 0)
    off = row0 // tm
    return pl.pallas_call(
        _combine_final_kernel, grid=(rows // tm,),
        in_specs=[_row_spec(tm, d, off), _row_spec(tm, d, off), _row_spec(tm, d, off + n // tm),
                  _row_spec(tm, LANES, off), _const_spec((1, d))],
        out_specs=_row_spec(tm, d),
        out_shape=jax.ShapeDtypeStruct((rows, d), F32),
        compiler_params=_cp(("parallel",)), name="combine_final",
    )(x, o, o, w, g.reshape(1, d))


def router(x, g, wr, br, n_groups, per_group):
    n, d = x.shape
    tm = _tile(n, (320, 256, 128, 64, 32, 16, 8))
    kern = functools.partial(_router_kernel, n_groups=n_groups, per_group=per_group)
    return pl.pallas_call(
        kern, grid=(n // tm,),
        in_specs=[_row_spec(tm, d), _const_spec((1, d)), _const_spec((d, LANES)), _const_spec((1, LANES))],
        out_specs=[_row_spec(tm, LANES), _row_spec(tm, LANES)],
        out_shape=[jax.ShapeDtypeStruct((n, LANES), I32), jax.ShapeDtypeStruct((n, LANES), F32)],
        compiler_params=_cp(("parallel",)), name="router",
    )(x, g.reshape(1, d), wr, br)


def _mm_kernel(a_ref, b_ref, o_ref):
    o_ref[...] = jnp.dot(a_ref[...], b_ref[...], preferred_element_type=F32).astype(o_ref.dtype)


def _mm_res_kernel(a_ref, b_ref, r_ref, o_ref):
    o_ref[...] = r_ref[...] + jnp.dot(a_ref[...], b_ref[...], preferred_element_type=F32)


MM_TM = (1024, 832, 640, 512, 256, 128, 64, 32, 16)
MM_TN = (1024, 512, 256, 128)


def matmul(a, b, out_dtype=F32, n=None):
    m, k = a.shape
    n = b.shape[1] if n is None else n
    tm, tn = _tile(m, MM_TM), _tile(n, MM_TN)
    return pl.pallas_call(
        _mm_kernel, grid=(m // tm, n // tn),
        in_specs=[pl.BlockSpec((tm, k), lambda i, j: (i, 0)), pl.BlockSpec((k, tn), lambda i, j: (0, j))],
        out_specs=pl.BlockSpec((tm, tn), lambda i, j: (i, j)),
        out_shape=jax.ShapeDtypeStruct((m, n), out_dtype),
        compiler_params=_cp(("parallel", "parallel")), name="matmul",
    )(a, b)


def _mm_nt_kernel(a_ref, bt_ref, o_ref):
    o_ref[...] = lax.dot_general(a_ref[...], bt_ref[...], (((1,), (1,)), ((), ())),
                                 preferred_element_type=F32).astype(o_ref.dtype)


def matmul_nt(a, bt, out_dtype=F32, n=None):
    m, k = a.shape
    n = bt.shape[0] if n is None else n
    tm, tn = _tile(m, MM_TM), _tile(n, MM_TN)
    return pl.pallas_call(
        _mm_nt_kernel, grid=(m // tm, n // tn),
        in_specs=[pl.BlockSpec((tm, k), lambda i, j: (i, 0)), pl.BlockSpec((tn, k), lambda i, j: (j, 0))],
        out_specs=pl.BlockSpec((tm, tn), lambda i, j: (i, j)),
        out_shape=jax.ShapeDtypeStruct((m, n), out_dtype),
        compiler_params=_cp(("parallel", "parallel")), name="matmul_nt",
    )(a, bt)


def matmul_add_rows(a, b, x, row0):
    m, k = a.shape
    _, n = b.shape
    tm, tn = _tile(m, MM_TM), _tile(n, MM_TN)
    assert row0 % tm == 0
    off = row0 // tm
    xs = pl.BlockSpec((tm, tn), lambda i, j: (i + off, j))
    return pl.pallas_call(
        _mm_res_kernel, grid=(m // tm, n // tn),
        in_specs=[pl.BlockSpec((tm, k), lambda i, j: (i, 0)), pl.BlockSpec((k, tn), lambda i, j: (0, j)), xs],
        out_specs=xs,
        out_shape=jax.ShapeDtypeStruct(x.shape, F32),
        input_output_aliases={2: 0},
        compiler_params=_cp(("parallel", "parallel")), name="matmul_add_rows",
    )(a, b, x)


def _lru_gates(xc, wai_ref, ba_ref, bx_ref, lam_ref, a_out, b_out):
    n_heads, hd, _ = wai_ref.shape
    for h in range(n_heads):
        sl = slice(h * hd, (h + 1) * hd)
        xh = xc[:, sl]
        ri = jnp.dot(xh.astype(BF16), wai_ref[h], preferred_element_type=F32)
        r = _sigmoid(ri[:, :hd] + ba_ref[:, sl])
        i = _sigmoid(ri[:, hd:] + bx_ref[:, sl])
        log_a = (-LRU_C) * r * _softplus(-lam_ref[:, sl])
        a_out[:, sl] = jnp.exp(log_a)
        b_out[:, sl] = jnp.sqrt(_one_minus_exp2(log_a)) * (i * xh)


def _even_prompt_kernel(gb_ref, gc_ref, xa_ref, xr_ref, gr_ref, caw_ref, cbw_ref, cbb_ref, wai_ref,
                        ba_ref, bx_ref, lam_ref, y_ref, pa_ref, pb_ref, ph_ref,
                        ubuf, rbuf, hcar, abuf, bbuf, hbuf):
    t = pl.program_id(1)
    tt, w = gb_ref.shape
    ka, kb = caw_ref.shape[0], cbw_ref.shape[0]

    @pl.when(t == 0)
    def _():
        ubuf[0:8, :] = jnp.zeros((8, w), F32)
        rbuf[0:8, :] = jnp.zeros((8, w), F32)
        hcar[...] = jnp.zeros_like(hcar)

    u = gc_ref[...] * xa_ref[...]
    ubuf[8:8 + tt, :] = u
    v = caw_ref[ka - 1:ka, :] * u
    for k in range(1, ka):
        v = v + caw_ref[ka - 1 - k:ka - k, :] * ubuf[8 - k:8 - k + tt, :]
    y_ref[:, 0:w] = (gb_ref[...] * v).astype(y_ref.dtype)

    xr = xr_ref[...]
    rbuf[8:8 + tt, :] = xr
    xc = cbw_ref[kb - 1:kb, :] * xr + cbb_ref[...]
    for k in range(1, kb):
        xc = xc + cbw_ref[kb - 1 - k:kb - k, :] * rbuf[8 - k:8 - k + tt, :]
    _lru_gates(xc, wai_ref, ba_ref, bx_ref, lam_ref, abuf, bbuf)

    def step(i, h):
        h = abuf[pl.ds(i, 1), :] * h + bbuf[pl.ds(i, 1), :]
        hbuf[pl.ds(i, 1), :] = h
        return h

    h = lax.fori_loop(0, tt, step, hcar[0:1, :], unroll=8)
    hcar[0:1, :] = h
    y_ref[:, w:2 * w] = (_gelu_tanh(gr_ref[...]) * hbuf[...]).astype(y_ref.dtype)

    pa_ref[0] = ubuf[8 + tt - (ka - 1):8 + tt, :]
    pb_ref[0] = rbuf[8 + tt - (kb - 1):8 + tt, :]
    ph_ref[0] = h
    ubuf[0:8, :] = ubuf[tt:tt + 8, :]
    rbuf[0:8, :] = rbuf[tt:tt + 8, :]


def even_prompt(z, n_b, n_t, caw, cbw, cbb, wai, ba, bx, lam):
    n = n_b * n_t
    w = z.shape[1] // 5
    tt = _tile(n_t, (128, 64, 32, 16, 8))
    nt = n_t // tt
    ka, kb = caw.shape[0], cbw.shape[0]

    def zs(c):
        return pl.BlockSpec((tt, w), lambda b, t: (b * nt + t, c))

    return pl.pallas_call(
        _even_prompt_kernel, grid=(n_b, nt),
        in_specs=[zs(0), zs(1), zs(2), zs(3), zs(4), _const_spec((ka, w)), _const_spec((kb, w)),
                  _const_spec((1, w)), _const_spec(wai.shape), _const_spec((1, w)), _const_spec((1, w)),
                  _const_spec((1, w))],
        out_specs=[pl.BlockSpec((tt, 2 * w), lambda b, t: (b * nt + t, 0)),
                   pl.BlockSpec((1, ka - 1, w), lambda b, t: (b, 0, 0)),
                   pl.BlockSpec((1, kb - 1, w), lambda b, t: (b, 0, 0)),
                   pl.BlockSpec((1, 1, w), lambda b, t: (b, 0, 0))],
        out_shape=[jax.ShapeDtypeStruct((n, 2 * w), BF16),
                   jax.ShapeDtypeStruct((n_b, ka - 1, w), F32),
                   jax.ShapeDtypeStruct((n_b, kb - 1, w), F32),
                   jax.ShapeDtypeStruct((n_b, 1, w), F32)],
        scratch_shapes=[pltpu.VMEM((8 + tt, w), F32), pltpu.VMEM((8 + tt, w), F32), pltpu.VMEM((8, w), F32),
                        pltpu.VMEM((tt, w), F32), pltpu.VMEM((tt, w), F32), pltpu.VMEM((tt, w), F32)],
        compiler_params=_cp(("parallel", "arbitrary")), name="even_prompt",
    )(z, z, z, z, z, caw, cbw, cbb.reshape(1, w), wai, ba.reshape(1, w), bx.reshape(1, w), lam.reshape(1, w))


def _even_sample_kernel(gb_ref, gc_ref, xa_ref, xr_ref, gr_ref, sa_ref, sb_ref, sh_ref, caw_ref, cbw_ref,
                        cbb_ref, wai_ref, ba_ref, bx_ref, lam_ref, y_ref, na_ref, nb_ref, nh_ref,
                        abuf, bbuf):
    w = gb_ref.shape[1]
    ka, kb = caw_ref.shape[0], cbw_ref.shape[0]
    u = gc_ref[...] * xa_ref[...]
    v = caw_ref[ka - 1:ka, :] * u
    for k in range(1, ka):
        v = v + caw_ref[ka - 1 - k:ka - k, :] * sa_ref[:, (ka - 1 - k) * w:(ka - k) * w]
    y_ref[:, 0:w] = (gb_ref[...] * v).astype(y_ref.dtype)
    for k in range(ka - 2):
        na_ref[:, k * w:(k + 1) * w] = sa_ref[:, (k + 1) * w:(k + 2) * w]
    na_ref[:, (ka - 2) * w:(ka - 1) * w] = u

    xr = xr_ref[...]
    xc = cbw_ref[kb - 1:kb, :] * xr + cbb_ref[...]
    for k in range(1, kb):
        xc = xc + cbw_ref[kb - 1 - k:kb - k, :] * sb_ref[:, (kb - 1 - k) * w:(kb - k) * w]
    for k in range(kb - 2):
        nb_ref[:, k * w:(k + 1) * w] = sb_ref[:, (k + 1) * w:(k + 2) * w]
    nb_ref[:, (kb - 2) * w:(kb - 1) * w] = xr

    _lru_gates(xc, wai_ref, ba_ref, bx_ref, lam_ref, abuf, bbuf)
    h = abuf[...] * sh_ref[...] + bbuf[...]
    nh_ref[...] = h
    y_ref[:, w:2 * w] = (_gelu_tanh(gr_ref[...]) * h).astype(y_ref.dtype)


def even_sample(z, row0, sa, sb, sh, caw, cbw, cbb, wai, ba, bx, lam):
    s, w = sh.shape
    ka, kb = caw.shape[0], cbw.shape[0]
    assert row0 % s == 0
    rb = row0 // s

    def zs(c):
        return pl.BlockSpec((s, w), lambda i: (rb, c))

    full = lambda shape: pl.BlockSpec(shape, lambda i: tuple(0 for _ in shape))
    return pl.pallas_call(
        _even_sample_kernel, grid=(1,),
        in_specs=[zs(0), zs(1), zs(2), zs(3), zs(4), full(sa.shape), full(sb.shape), full(sh.shape),
                  full((ka, w)), full((kb, w)), full((1, w)), full(wai.shape), full((1, w)), full((1, w)),
                  full((1, w))],
        out_specs=[full((s, 2 * w)), full(sa.shape), full(sb.shape), full(sh.shape)],
        out_shape=[jax.ShapeDtypeStruct((s, 2 * w), BF16), jax.ShapeDtypeStruct(sa.shape, F32),
                   jax.ShapeDtypeStruct(sb.shape, F32), jax.ShapeDtypeStruct(sh.shape, F32)],
        scratch_shapes=[pltpu.VMEM((s, w), F32), pltpu.VMEM((s, w), F32)],
        compiler_params=_cp(("arbitrary",)), name="even_sample",
    )(z, z, z, z, z, sa, sb, sh, caw, cbw, cbb.reshape(1, w), wai, ba.reshape(1, w), bx.reshape(1, w),
      lam.reshape(1, w))


def _gla_log_decay(gr_ref, wa2_ref, ba_ref):
    pre = jnp.dot(gr_ref[...].astype(BF16), wa2_ref[...], preferred_element_type=F32) + ba_ref[...]
    return _log_sigmoid(pre) * (1.0 / GLA_GATE_NORM)


def _head_norm_gate(o, gn_ref, g):
    o = o * lax.rsqrt(jnp.mean(o * o, axis=-1, keepdims=True) + EPS) * gn_ref[...]
    return o * (g * _sigmoid(g))


def _cumsum_rows(tri16, x):
    hi = x.astype(BF16)
    r1 = x - hi.astype(F32)
    mid = r1.astype(BF16)
    lo = (r1 - mid.astype(F32)).astype(BF16)
    dot = lambda t: jnp.dot(tri16, t, preferred_element_type=F32)
    return dot(hi) + dot(mid) + dot(lo)


GLA_SUB = 16
GLA_SAFE_LOG = 60.0


def _gla_rows(r0, cs, q_ref, k_ref, v_ref, g_ref, gn_ref, o_ref, gabuf, st):
    n_h, dv, dk = st.shape
    rows = pl.ds(r0, cs)
    row = lax.broadcasted_iota(I32, (cs, cs), 0)
    col = lax.broadcasted_iota(I32, (cs, cs), 1)
    causal = row >= col
    b = _cumsum_rows(causal.astype(BF16), gabuf[rows, :])
    b_last = b[cs - 1:cs, :]
    k = k_ref[rows, :]
    qe_all = (q_ref[rows, :] * (dk ** -0.5) * jnp.exp(b)).astype(BF16)
    ke_all = (k * jnp.exp(-b)).astype(BF16)
    kd_all = (k * jnp.exp(b_last - b)).astype(BF16)
    decay = jnp.exp(b_last)
    nt = (((1,), (1,)), ((), ()))
    tn = (((0,), (0,)), ((), ()))
    for h in range(n_h):
        sk, sv = slice(h * dk, (h + 1) * dk), slice(h * dv, (h + 1) * dv)
        qe, ke, kd = qe_all[:, sk], ke_all[:, sk], kd_all[:, sk]
        v16 = v_ref[rows, sv].astype(BF16)
        att = lax.dot_general(qe, ke, nt, preferred_element_type=F32)
        att = jnp.where(causal, att, 0.0).astype(BF16)
        s_old = st[h]
        o = lax.dot_general(qe, s_old.astype(BF16), nt, preferred_element_type=F32)
        o = o + jnp.dot(att, v16, preferred_element_type=F32)
        st[h] = s_old * decay[:, sk] + lax.dot_general(v16, kd, tn, preferred_element_type=F32)
        o_ref[rows, sv] = _head_norm_gate(o, gn_ref, g_ref[rows, sv]).astype(o_ref.dtype)


def _gla_prompt_kernel(q_ref, k_ref, v_ref, g_ref, gr_ref, wa2_ref, ba_ref, gn_ref, o_ref, s_ref, gabuf, st):
    c = pl.program_id(1)
    cs = q_ref.shape[0]

    @pl.when(c == 0)
    def _():
        st[...] = jnp.zeros_like(st)

    ga = _gla_log_decay(gr_ref, wa2_ref, ba_ref)
    gabuf[...] = ga
    one_piece = jnp.min(jnp.sum(ga, axis=0, keepdims=True)) > -GLA_SAFE_LOG
    args = (q_ref, k_ref, v_ref, g_ref, gn_ref, o_ref, gabuf, st)

    @pl.when(one_piece)
    def _():
        _gla_rows(0, cs, *args)

    @pl.when(jnp.logical_not(one_piece))
    def _():
        def sub(j, carry):
            _gla_rows(pl.multiple_of(j * GLA_SUB, GLA_SUB), GLA_SUB, *args)
            return carry
        lax.fori_loop(0, cs // GLA_SUB, sub, 0)

    @pl.when(c == pl.num_programs(1) - 1)
    def _():
        for h in range(st.shape[0]):
            s_ref[0, h] = st[h].T


def gla_prompt(z, zg, n_b, n_t, n_h, dk, dv, wa2p, ba, gnorm):
    n = n_b * n_t
    cs = _tile(n_t, (256, 128, 64, 32, 16))
    nc = n_t // cs
    dkt, dvt = n_h * dk, n_h * dv
    assert (2 * dkt) % dvt == 0 and cs % GLA_SUB == 0
    vb0 = (2 * dkt) // dvt
    row = lambda b, c: b * nc + c
    return pl.pallas_call(
        _gla_prompt_kernel, grid=(n_b, nc),
        in_specs=[pl.BlockSpec((cs, dkt), lambda b, c: (row(b, c), 0)),
                  pl.BlockSpec((cs, dkt), lambda b, c: (row(b, c), 1)),
                  pl.BlockSpec((cs, dvt), lambda b, c: (row(b, c), vb0)),
                  pl.BlockSpec((cs, dvt), lambda b, c: (row(b, c), vb0 + 1)),
                  pl.BlockSpec((cs, LANES), lambda b, c: (row(b, c), 0)),
                  _const_spec((LANES, dkt)), _const_spec((1, dkt)), _const_spec((1, dv))],
        out_specs=[pl.BlockSpec((cs, dvt), lambda b, c: (row(b, c), 0)),
                   pl.BlockSpec((1, n_h, dk, dv), lambda b, c: (b, 0, 0, 0))],
        out_shape=[jax.ShapeDtypeStruct((n, dvt), BF16),
                   jax.ShapeDtypeStruct((n_b, n_h, dk, dv), F32)],
        scratch_shapes=[pltpu.VMEM((cs, dkt), F32), pltpu.VMEM((n_h, dv, dk), F32)],
        compiler_params=_cp(("parallel", "arbitrary")), name="gla_prompt",
    )(z, z, z, z, zg, wa2p, ba.reshape(1, -1), gnorm.reshape(1, dv))


def _gla_sample_kernel(q_ref, k_ref, v_ref, g_ref, gr_ref, wa2_ref, ba_ref, gn_ref, s0_ref,
                       o_ref, s_ref, ta, tk, tq):
    i = pl.program_id(1)
    nblk, dk, bb = ta.shape

    @pl.when(i == 0)
    def _():
        alpha_t = jnp.exp(_gla_log_decay(gr_ref, wa2_ref, ba_ref)).T
        k_t = k_ref[...].T
        q_t = (q_ref[...] * (dk ** -0.5)).T
        for j in range(nblk):
            ta[j] = alpha_t[:, j * bb:(j + 1) * bb]
            tk[j] = k_t[:, j * bb:(j + 1) * bb]
            tq[j] = q_t[:, j * bb:(j + 1) * bb]

    a3, k3, q3 = ta[i], tk[i], tq[i]
    v = v_ref[...]
    rows = []
    for j in range(bb):
        s_new = s0_ref[j, 0] * a3[:, j:j + 1] + k3[:, j:j + 1] * v[j:j + 1, :]
        s_ref[j, 0] = s_new
        rows.append(jnp.sum(q3[:, j:j + 1] * s_new, axis=0, keepdims=True))
    o = jnp.concatenate(rows, axis=0)
    o_ref[...] = _head_norm_gate(o, gn_ref, g_ref[...]).astype(o_ref.dtype)


def gla_sample(z, zg, row0, s0, wa2p, ba, gnorm):
    s, n_h, dk, dv = s0.shape
    bb = _tile(s, (16, 8))
    nblk = s // bb
    assert row0 % s == 0 and (2 * n_h * dk) % dv == 0
    rs, rb = row0 // s, row0 // bb
    kb0 = n_h
    vb0 = (2 * n_h * dk) // dv
    gb0 = vb0 + n_h
    return pl.pallas_call(
        _gla_sample_kernel, grid=(n_h, nblk),
        in_specs=[pl.BlockSpec((s, dk), lambda h, i: (rs, h)),
                  pl.BlockSpec((s, dk), lambda h, i: (rs, kb0 + h)),
                  pl.BlockSpec((bb, dv), lambda h, i: (rb + i, vb0 + h)),
                  pl.BlockSpec((bb, dv), lambda h, i: (rb + i, gb0 + h)),
                  pl.BlockSpec((s, LANES), lambda h, i: (rs, 0)),
                  pl.BlockSpec((LANES, dk), lambda h, i: (0, h)),
                  pl.BlockSpec((1, dk), lambda h, i: (0, h)),
                  pl.BlockSpec((1, dv), lambda h, i: (0, 0)),
                  pl.BlockSpec((bb, 1, dk, dv), lambda h, i: (i, h, 0, 0))],
        out_specs=[pl.BlockSpec((bb, dv), lambda h, i: (i, h)),
                   pl.BlockSpec((bb, 1, dk, dv), lambda h, i: (i, h, 0, 0))],
        out_shape=[jax.ShapeDtypeStruct((s, n_h * dv), BF16), jax.ShapeDtypeStruct(s0.shape, F32)],
        scratch_shapes=[pltpu.VMEM((nblk, dk, bb), F32)] * 3,
        compiler_params=_cp(("arbitrary", "arbitrary")), name="gla_sample",
    )(z, z, z, z, zg, wa2p, ba.reshape(1, -1), gnorm.reshape(1, dv), s0)


MOE_SHORT_TILE = 128
MOE_AHEAD = 4
MOE_NORM_ROWS = 32
MOE_ROW_UNROLL = 8
MOE_WEIGHT_DMA_PRIORITY = 1


def _moe_kernel(ie_ref, istart_ref, icnt_ref, tok_ref, dst_ref, xn_hbm, gain_ref, wg_hbm, wu_hbm, wd_hbm, o_hbm,
                xstage, obuf, xb16, hbuf, ring_up, ring_dn, wg16, wu16, wd16,
                sem_x, sem_o, sem_up, sem_dn, *, layer, tiles):
    it = pl.program_id(0)
    last = pl.num_programs(0) - 1
    e, start, cnt = ie_ref[it], istart_ref[it], icnt_ref[it]
    nxt = jnp.minimum(it + 1, last)
    has_next = (it < last) & (icnt_ref[nxt] > 0)
    e_next, start_next, cnt_next = ie_ref[nxt], istart_ref[nxt], icnt_ref[nxt]
    prv = jnp.maximum(it - 1, 0)
    start_prev, cnt_prev = istart_ref[prv], jnp.where(it > 0, icnt_ref[prv], 0)
    fc, dc = wg16.shape[1], wd16.shape[1]
    nf, nd = hbuf.shape[1] // fc, obuf.shape[1] // dc
    n_chunks = 2 * nf + nd
    up_slots, dn_slots = ring_up.shape[0], ring_dn.shape[0]
    ahead = dn_slots
    big, tails = tiles[0], tiles[1:]

    def chunk_copy(k, ex):
        if k < 2 * nf:
            w_hbm = wg_hbm if k % 2 == 0 else wu_hbm
            slot = k % up_slots
            return pltpu.make_async_copy(w_hbm.at[layer, ex, :, pl.ds((k // 2) * fc, fc)], ring_up.at[slot],
                                         sem_up.at[slot])
        c = k - 2 * nf
        slot = c % dn_slots
        return pltpu.make_async_copy(wd_hbm.at[layer, ex, :, pl.ds(c * dc, dc)], ring_dn.at[slot],
                                     sem_dn.at[slot])

    def prefetch(ks):
        for k in ks:
            if k < n_chunks:
                chunk_copy(k, e).start(priority=MOE_WEIGHT_DMA_PRIORITY)
            else:
                @pl.when(has_next)
                def _(k=k):
                    chunk_copy(k - n_chunks, e_next).start(priority=MOE_WEIGHT_DMA_PRIORITY)

    def gather_copy(r, base):
        return pltpu.make_async_copy(xn_hbm.at[pl.ds(tok_ref[base + r], 1), :], xstage.at[pl.ds(r, 1), :], sem_x)

    def scatter_copy(r, base):
        return pltpu.make_async_copy(obuf.at[pl.ds(r, 1), :], o_hbm.at[pl.ds(dst_ref[base + r], 1), :], sem_o)

    def for_rows(n, fn):
        n_grp = lax.shift_right_logical(n, MOE_ROW_UNROLL.bit_length() - 1)

        def group(g, c):
            for u in range(MOE_ROW_UNROLL):
                fn(pl.multiple_of(g * MOE_ROW_UNROLL, MOE_ROW_UNROLL) + u)
            return c
        lax.fori_loop(0, n_grp, group, 0)

        def single(r, c):
            fn(r)
            return c
        lax.fori_loop(n_grp * MOE_ROW_UNROLL, n, single, 0)

    def for_tiles(fn):
        rem = cnt % big
        n_big = cnt // big + (rem > (tails[0] if tails else 0)).astype(I32)

        def body(j, c):
            fn(pl.multiple_of(j * big, big), big)
            return c
        lax.fori_loop(0, n_big, body, 0)

        for t, t_below in zip(tails, tails[1:] + (0,)):
            @pl.when((rem > t_below) & (rem <= t))
            def _(t=t):
                fn(pl.multiple_of(n_big * big, big), t)

    @pl.when(it == 0)
    def _():
        xstage[...] = jnp.zeros_like(xstage)

    @pl.when((it == 0) & (cnt > 0))
    def _():
        prefetch(range(ahead))
        for_rows(cnt, lambda r: gather_copy(r, start).start())

    @pl.when(cnt > 0)
    def _():
        for_rows(cnt, lambda r: gather_copy(r, start).wait())

        def cast(r0, t):
            for s in range(0, t, MOE_NORM_ROWS):
                rows = pl.ds(r0 + s, MOE_NORM_ROWS)
                xb16[rows, :] = _rms(xstage[rows, :], gain_ref[...]).astype(BF16)
        for_tiles(cast)

        @pl.when(has_next)
        def _():
            for_rows(cnt_next, lambda r: gather_copy(r, start_next).start())

        for c in range(nf):
            chunk_copy(2 * c, e).wait()
            wg16[...] = ring_up[(2 * c) % up_slots].astype(BF16)
            chunk_copy(2 * c + 1, e).wait()
            wu16[...] = ring_up[(2 * c + 1) % up_slots].astype(BF16)
            prefetch((2 * c + ahead, 2 * c + ahead + 1))

            def up(r0, t, c=c):
                x = xb16[pl.ds(r0, t), :]
                g = jnp.dot(x, wg16[...], preferred_element_type=F32)
                u = jnp.dot(x, wu16[...], preferred_element_type=F32)
                hbuf[pl.ds(r0, t), c * fc:(c + 1) * fc] = (g * _sigmoid(g) * u).astype(BF16)
            for_tiles(up)

        for_rows(cnt_prev, lambda r: scatter_copy(r, start_prev).wait())

        for c in range(nd):
            k = 2 * nf + c
            chunk_copy(k, e).wait()
            wd16[...] = ring_dn[c % dn_slots].astype(BF16)
            prefetch((k + ahead,))

            def down(r0, t, c=c):
                obuf[pl.ds(r0, t), c * dc:(c + 1) * dc] = jnp.dot(hbuf[pl.ds(r0, t), :], wd16[...],
                                                                 preferred_element_type=F32)
            for_tiles(down)

        for_rows(cnt, lambda r: scatter_copy(r, start).start())

        @pl.when(jnp.logical_not(has_next))
        def _():
            for_rows(cnt, lambda r: scatter_copy(r, start).wait())


def moe_route_tables(eid, n_exp, rows):
    n = eid.shape[0]
    flat = eid.reshape(-1)
    na = flat.shape[0]
    order = jnp.argsort(flat).astype(I32)
    counts = jnp.bincount(flat, length=n_exp).astype(I32)
    gstart = jnp.cumsum(counts) - counts
    n_it = (counts + rows - 1) // rows
    it_cum = jnp.cumsum(n_it)
    it_base = it_cum - n_it
    total = it_cum[-1]
    n_items = n_exp + na // rows
    ii = jnp.arange(n_items, dtype=I32)
    e_i = jnp.minimum(jnp.searchsorted(it_cum, ii, side="right").astype(I32), n_exp - 1)
    valid = ii < total
    e_i = jnp.where(valid, e_i, e_i[jnp.maximum(total - 1, 0)])
    j = ii - it_base[e_i]
    start = jnp.where(valid, gstart[e_i] + j * rows, 0).astype(I32)
    cnt = jnp.where(valid, jnp.minimum(rows, counts[e_i] - j * rows), 0).astype(I32)
    tok = order // TOP_K
    dst = (order % TOP_K) * n + tok
    return e_i, start, cnt, tok, dst


def _moe_tiles(n_assign, n_exp):
    mean = n_assign / n_exp
    big = max(MOE_SHORT_TILE, -(-int(mean + 2.0 * mean ** 0.5) // 32) * 32)
    return (big, MOE_SHORT_TILE) if big > MOE_SHORT_TILE else (big,)


def moe_ffn(xn, gain, eid, wg, wu, wd, layer):
    n, d = xn.shape
    _, n_exp, _, f = wg.shape
    tiles = _moe_tiles(TOP_K * n, n_exp)
    rows = tiles[0]
    fc = _tile(f, (256, 128))
    dc = _tile(d, (1024, 512, 256, 128))
    nf, nd = f // fc, d // dc
    ahead = min(MOE_AHEAD, 2 * nf + nd, nd + 1)
    up_slots, dn_slots = ahead + 1, ahead
    e_i, start, cnt, tok, dst = moe_route_tables(eid, n_exp, rows)
    n_items = e_i.shape[0]
    kern = functools.partial(_moe_kernel, layer=layer, tiles=tiles)
    any_spec = pl.BlockSpec(memory_space=pl.ANY)
    grid_spec = pltpu.PrefetchScalarGridSpec(
        num_scalar_prefetch=5, grid=(n_items,),
        in_specs=[any_spec, pl.BlockSpec((1, d), lambda i, *_: (0, 0)), any_spec, any_spec, any_spec],
        out_specs=any_spec,
        scratch_shapes=[pltpu.VMEM((rows, d), F32), pltpu.VMEM((rows, d), F32), pltpu.VMEM((rows, d), BF16),
                        pltpu.VMEM((rows, f), BF16),
                        pltpu.VMEM((up_slots, d, fc), F32), pltpu.VMEM((dn_slots, f, dc), F32),
                        pltpu.VMEM((d, fc), BF16), pltpu.VMEM((d, fc), BF16), pltpu.VMEM((f, dc), BF16),
                        pltpu.SemaphoreType.DMA(()), pltpu.SemaphoreType.DMA(()),
                        pltpu.SemaphoreType.DMA((up_slots,)), pltpu.SemaphoreType.DMA((dn_slots,))])
    return pl.pallas_call(
        kern, grid_spec=grid_spec,
        out_shape=jax.ShapeDtypeStruct((TOP_K * n, d), F32),
        compiler_params=_cp(("arbitrary",)), name="moe_ffn",
    )(e_i, start, cnt, tok, dst, xn, gain.reshape(1, d), wg, wu, wd)


def moe_layer(x, g, wrg, brg, wre, bre, wg, wu, wd, layer):
    n_groups = wrg.shape[1]
    n_exp = wre.shape[1]
    d = x.shape[1]
    wr = jnp.zeros((d, LANES), BF16).at[:, :n_groups].set(wrg.astype(BF16))
    wr = wr.at[:, n_groups:n_groups + n_exp].set(wre.astype(BF16))
    br = jnp.zeros((1, LANES), F32).at[0, :n_groups].set(brg).at[0, n_groups:n_groups + n_exp].set(bre)
    eid, wts = router(x, g, wr, br, n_groups, n_exp // n_groups)
    o = moe_ffn(x, g, eid[:, :TOP_K], wg, wu, wd, layer)
    return o, wts


def kernel(x_prompt, x_sample, state_conv_a, state_conv_b, state_lru_h, state_gla, norm_mix, norm_ffn,
           norm_final, w_in_even, conv_a_w, conv_b_w, conv_b_b, lru_wa, lru_ba, lru_wx, lru_bx, lru_lambda,
           w_out_even, w_in_odd, gla_wa2, gla_ba, gla_norm, w_out_odd, router_group_w, router_group_b,
           router_expert_w, router_expert_b, moe_w_gate, moe_w_up, moe_w_down):
    pb, pt, d = x_prompt.shape
    sb, st_, _ = x_sample.shape
    assert st_ == 1
    n_p = pb * pt
    n = n_p + sb
    depth = norm_mix.shape[0]
    w = state_lru_h.shape[-1]
    n_h, dk, dv = state_gla.shape[2:]

    new = {k: [] for k in ("pa", "pb", "ph", "ps", "sa", "sb", "sh", "ss")}
    pending = None

    for l in range(depth):
        e = l // 2
        if pending is None:
            x, xn = stack_norm(x_prompt.reshape(n_p, d), x_sample.reshape(sb, d), norm_mix[l])
        else:
            x, xn = combine_norm(x, pending[0], pending[1], norm_mix[l])
        if l % 2 == 0:
            z = matmul(xn, w_in_even[e].astype(BF16))
            wai = jnp.concatenate([lru_wa[e], lru_wx[e]], axis=-1).astype(BF16)
            mix = (conv_a_w[e], conv_b_w[e], conv_b_b[e], wai, lru_ba[e], lru_bx[e], lru_lambda[e])
            y_pr, p_a, p_b, p_h = even_prompt(z, pb, pt, *mix)
            ka, kb = conv_a_w.shape[1], conv_b_w.shape[1]
            y_sm, s_a, s_b, s_h = even_sample(z, n_p, state_conv_a[e].reshape(sb, (ka - 1) * w),
                                              state_conv_b[e].reshape(sb, (kb - 1) * w), state_lru_h[e], *mix)
            new["pa"].append(p_a)
            new["pb"].append(p_b)
            new["ph"].append(p_h.reshape(pb, w))
            new["sa"].append(s_a.reshape(sb, ka - 1, w))
            new["sb"].append(s_b.reshape(sb, kb - 1, w))
            new["sh"].append(s_h)
            w_out = w_out_even[e].astype(BF16)
        else:
            n_main = 2 * n_h * dk + 2 * n_h * dv
            rank = w_in_odd.shape[2] - n_main
            w_in_t = jnp.swapaxes(w_in_odd[e], 0, 1).astype(BF16)
            z = matmul_nt(xn, w_in_t, n=n_main)
            w_gate_t = jnp.zeros((LANES, d), BF16).at[:rank].set(w_in_t[n_main:])
            zg = matmul_nt(xn, w_gate_t)
            wa2p = jnp.zeros((LANES, n_h * dk), BF16).at[:rank].set(gla_wa2[e].astype(BF16))
            y_pr, p_s = gla_prompt(z, zg, pb, pt, n_h, dk, dv, wa2p, gla_ba[e], gla_norm[e])
            y_sm, s_s = gla_sample(z, zg, n_p, state_gla[e], wa2p, gla_ba[e], gla_norm[e])
            new["ps"].append(p_s)
            new["ss"].append(s_s)
            w_out = w_out_odd[e].astype(BF16)
        x = matmul_add_rows(y_pr, w_out, x, 0)
        x = matmul_add_rows(y_sm, w_out, x, n_p)
        pending = moe_layer(x, norm_ffn[l], router_group_w[l], router_group_b[l], router_expert_w[l],
                            router_expert_b[l], moe_w_gate, moe_w_up, moe_w_down, l)

    y_p = combine_final(x, pending[0], pending[1], norm_final, 0, n_p).reshape(pb, pt, d)
    y_s = combine_final(x, pending[0], pending[1], norm_final, n_p, sb).reshape(sb, 1, d)
    stack = lambda k: jnp.stack(new[k])
    return (y_p, y_s, stack("pa"), stack("pb"), stack("ph"), stack("ps"),
            stack("sa"), stack("sb"), stack("sh"), stack("ss"))
```
